```python
import math
import jax, jax.numpy as jnp
from jax import lax
import numpy as np

D_MODEL = 4096
BATCH = 2
SEQ = 8192
DEPTH = 2

CONV_CH = D_MODEL // 2
CONV_HEADS = 16
CONV_K = 3
HG_HEADS = 16
HG_DK = (D_MODEL // 2) // HG_HEADS
HG_DV = HG_DK
HG_WIDTH = HG_HEADS * HG_DK
HG_CHUNK = 32
EVEN_MIX = CONV_CH + HG_WIDTH
EVEN_IN = 3 * CONV_CH + 3 * HG_WIDTH + EVEN_MIX
S5_WIDTH = D_MODEL
S5_GROUP = 16
S5_GROUPS = S5_WIDTH // S5_GROUP
S5_STATE = 64
S5_CHUNK = 128
ODD_IN = 2 * S5_WIDTH
N_EVEN = (DEPTH + 1) // 2
N_ODD = DEPTH // 2
ALPHA = (2 * DEPTH) ** 0.25
BETA = (8 * DEPTH) ** -0.25
LN_EPS = 1e-5
RMS_EPS = 1e-6
LAMBDA_RE_MAX = -1e-4

kernel_name = "hybrid_shortconv_hgrn2_s5_deepnorm"


def layer_norm(x, g, b):
    x32 = x.astype(jnp.float32)
    mu = jnp.mean(x32, axis=-1, keepdims=True)
    xc = x32 - mu
    var = jnp.mean(xc * xc, axis=-1, keepdims=True)
    return (xc * lax.rsqrt(var + LN_EPS) * g.astype(jnp.float32) + b.astype(jnp.float32)).astype(x.dtype)


def causal_short_conv(u, w):
    K = w.shape[0]
    S = u.shape[1]
    up = jnp.pad(u, ((0, 0), (K - 1, 0), (0, 0)))
    return sum(up[:, k:k + S, :] * w[k] for k in range(K))


def hgrn2_chunkwise(q, f_pre, v, lb):
    Bsz, S, _ = q.shape
    N = S // HG_CHUNK
    f = lb + (1.0 - lb) * jax.nn.sigmoid(f_pre)
    log_f = jnp.log(f)
    k = 1.0 - f

    def heads(t, d):
        return t.reshape(Bsz, N, HG_CHUNK, HG_HEADS, d).transpose(0, 3, 1, 2, 4)

    q, k, log_f, v = heads(q, HG_DK), heads(k, HG_DK), heads(log_f, HG_DK), heads(v, HG_DV)
    b = jnp.cumsum(log_f, axis=3)
    b_last = b[..., -1:, :]
    q_in = q * jnp.exp(b)
    k_in = k * jnp.exp(-b)
    k_dec = k * jnp.exp(b_last - b)
    chunk_decay = jnp.exp(b_last[..., 0, :])

    causal = jnp.tril(jnp.ones((HG_CHUNK, HG_CHUNK), dtype=bool))
    scores = jnp.einsum('bhncd,bhnsd->bhncs', q_in, k_in)
    scores = jnp.where(causal, scores, 0.0)
    o_intra = jnp.einsum('bhncs,bhnsv->bhncv', scores, v)

    def step(state, xs):
        q_c, kd_c, v_c, dec_c = xs
        o_c = jnp.einsum('bhcd,bhdv->bhcv', q_c, state)
        state = dec_c[..., None] * state + jnp.einsum('bhcd,bhcv->bhdv', kd_c, v_c)
        return state, o_c

    xs = (jnp.moveaxis(q_in, 2, 0), jnp.moveaxis(k_dec, 2, 0),
          jnp.moveaxis(v, 2, 0), jnp.moveaxis(chunk_decay, 2, 0))
    state0 = jnp.zeros((Bsz, HG_HEADS, HG_DK, HG_DV), jnp.float32)
    _, o_inter = lax.scan(step, state0, xs)
    o = o_intra + jnp.moveaxis(o_inter, 0, 2)
    return o.transpose(0, 2, 3, 1, 4).reshape(Bsz, S, HG_HEADS, HG_DV)


def even_mixer(x, w_in, conv_w, hg_norm, lb, w_out):
    Bsz, S, _ = x.shape
    proj = jnp.einsum('bsd,de->bse', x, w_in).astype(jnp.float32)
    c1, c2, c3 = CONV_CH, 2 * CONV_CH, 3 * CONV_CH
    h1, h2, h3 = c3 + HG_WIDTH, c3 + 2 * HG_WIDTH, c3 + 3 * HG_WIDTH
    a_b, a_c, a_h = proj[..., :c1], proj[..., c1:c2], proj[..., c2:c3]
    q, f_pre, v = proj[..., c3:h1], proj[..., h1:h2], proj[..., h2:h3]
    gate = proj[..., h3:]
    y_a = a_b * causal_short_conv(a_c * a_h, conv_w.astype(jnp.float32))
    o = hgrn2_chunkwise(q, f_pre, v, lb)
    o = o * lax.rsqrt(jnp.mean(o * o, axis=-1, keepdims=True) + RMS_EPS) * hg_norm.astype(jnp.float32)
    y_b = o.reshape(Bsz, S, HG_WIDTH)
    y = jnp.concatenate([y_a, y_b], axis=-1) * jax.nn.silu(gate)
    return jnp.einsum('bse,ed->bsd', y.astype(x.dtype), w_out)


def s5_binop(e1, e2):
    a1, b1 = e1
    a2, b2 = e2
    return a1 * a2, a2 * b1 + b2


def odd_mixer(x, w_in, lam_re, lam_im, log_step, b_re, b_im, c_re, c_im, d_skip, w_glu, b_glu, w_out):
    Bsz, S, _ = x.shape
    N = S // S5_CHUNK
    proj = jnp.einsum('bsd,de->bse', x, w_in).astype(jnp.float32)
    u, gate = proj[..., :S5_WIDTH], proj[..., S5_WIDTH:]
    lam = lax.complex(jnp.minimum(lam_re.astype(jnp.float32), LAMBDA_RE_MAX), lam_im.astype(jnp.float32))
    dt = jnp.exp(log_step.astype(jnp.float32))[:, None]
    lam_dt = lam * dt
    lam_bar = jnp.exp(lam_dt)
    b_bar = ((lam_bar - 1.0) / lam)[..., None] * lax.complex(b_re.astype(jnp.float32), b_im.astype(jnp.float32))
    b_bar_re, b_bar_im = jnp.real(b_bar), jnp.imag(b_bar)
    c_re32, c_im32 = c_re.astype(jnp.float32), c_im.astype(jnp.float32)
    lam_pow = jnp.exp(jnp.arange(1, S5_CHUNK + 1, dtype=jnp.float32)[:, None, None] * lam_dt)
    u_chunks = u.reshape(Bsz, N, S5_CHUNK, S5_GROUPS, S5_GROUP).transpose(1, 0, 2, 3, 4)

    def chunk_step(state, u_c):
        bu = lax.complex(jnp.einsum('blgc,gpc->blgp', u_c, b_bar_re),
                         jnp.einsum('blgc,gpc->blgp', u_c, b_bar_im))
        a = jnp.broadcast_to(lam_bar, bu.shape)
        _, xs = lax.associative_scan(s5_binop, (a, bu), axis=1)
        xs = xs + lam_pow[None] * state[:, None]
        y = (jnp.einsum('blgp,gcp->blgc', jnp.real(xs), c_re32)
             - jnp.einsum('blgp,gcp->blgc', jnp.imag(xs), c_im32))
        return xs[:, -1], y

    state0 = jnp.zeros((Bsz, S5_GROUPS, S5_STATE), jnp.complex64)
    _, y = lax.scan(chunk_step, state0, u_chunks)
    y = y.transpose(1, 0, 2, 3, 4).reshape(Bsz, S, S5_WIDTH)
    y = jax.nn.gelu(y + d_skip.astype(jnp.float32) * u)
    y = y * jax.nn.sigmoid(jnp.einsum('bse,ef->bsf', y, w_glu.astype(jnp.float32)) + b_glu.astype(jnp.float32))
    y = y * jax.nn.silu(gate)
    return jnp.einsum('bse,ed->bsd', y.astype(x.dtype), w_out)


def setup_inputs(seed: int = 0) -> dict:
    key = jax.random.key(seed)
    ks = jax.random.split(key, 24)
    f32 = jnp.float32
    nrm = lambda k, shape, s: jax.random.normal(k, shape, f32) * s
    x = jax.random.normal(ks[0], (BATCH, SEQ, D_MODEL), f32)
    ev_w_in = nrm(ks[1], (N_EVEN, D_MODEL, EVEN_IN), D_MODEL ** -0.5)
    ev_conv_w = nrm(ks[2], (N_EVEN, CONV_K, CONV_CH), CONV_K ** -0.5)
    ev_hg_norm = 1.0 + nrm(ks[3], (N_EVEN, HG_DV), 0.02)
    ev_w_out = nrm(ks[4], (N_EVEN, EVEN_MIX, D_MODEL), BETA * EVEN_MIX ** -0.5)
    ev_ln_g = 1.0 + nrm(ks[5], (N_EVEN, D_MODEL), 0.02)
    ev_ln_b = nrm(ks[6], (N_EVEN, D_MODEL), 0.02)
    hg_lb_logits = nrm(ks[7], (DEPTH + 1, HG_WIDTH), 0.1)
    od_w_in = nrm(ks[8], (N_ODD, D_MODEL, ODD_IN), D_MODEL ** -0.5)
    od_lam_re = -0.5 + nrm(ks[9], (N_ODD, S5_GROUPS, S5_STATE), 0.01)
    od_lam_im = (math.pi * jnp.arange(S5_STATE, dtype=f32))[None, None, :] + nrm(ks[10], (N_ODD, S5_GROUPS, S5_STATE), 0.01)
    od_log_step = jax.random.uniform(ks[11], (N_ODD, S5_GROUPS), f32, math.log(1e-3), math.log(1e-1))
    od_b_re = nrm(ks[12], (N_ODD, S5_GROUPS, S5_STATE, S5_GROUP), (2 * S5_GROUP) ** -0.5)
    od_b_im = nrm(ks[13], (N_ODD, S5_GROUPS, S5_STATE, S5_GROUP), (2 * S5_GROUP) ** -0.5)
    od_c_re = nrm(ks[14], (N_ODD, S5_GROUPS, S5_GROUP, S5_STATE), S5_STATE ** -0.5)
    od_c_im = nrm(ks[15], (N_ODD, S5_GROUPS, S5_GROUP, S5_STATE), S5_STATE ** -0.5)
    od_d = nrm(ks[16], (N_ODD, S5_WIDTH), 1.0)
    od_w_glu = nrm(ks[17], (N_ODD, S5_WIDTH, S5_WIDTH), S5_WIDTH ** -0.5)
    od_b_glu = nrm(ks[18], (N_ODD, S5_WIDTH), 0.02)
    od_w_out = nrm(ks[19], (N_ODD, S5_WIDTH, D_MODEL), BETA * S5_WIDTH ** -0.5)
    od_ln_g = 1.0 + nrm(ks[20], (N_ODD, D_MODEL), 0.02)
    od_ln_b = nrm(ks[21], (N_ODD, D_MODEL), 0.02)
    return {"x": x, "ev_w_in": ev_w_in, "ev_conv_w": ev_conv_w, "ev_hg_norm": ev_hg_norm,
            "ev_w_out": ev_w_out, "ev_ln_g": ev_ln_g, "ev_ln_b": ev_ln_b, "hg_lb_logits": hg_lb_logits,
            "od_w_in": od_w_in, "od_lam_re": od_lam_re, "od_lam_im": od_lam_im, "od_log_step": od_log_step,
            "od_b_re": od_b_re, "od_b_im": od_b_im, "od_c_re": od_c_re, "od_c_im": od_c_im,
            "od_d": od_d, "od_w_glu": od_w_glu, "od_b_glu": od_b_glu, "od_w_out": od_w_out,
            "od_ln_g": od_ln_g, "od_ln_b": od_ln_b}


def reference(x, ev_w_in, ev_conv_w, ev_hg_norm, ev_w_out, ev_ln_g, ev_ln_b, hg_lb_logits,
              od_w_in, od_lam_re, od_lam_im, od_log_step, od_b_re, od_b_im, od_c_re, od_c_im,
              od_d, od_w_glu, od_b_glu, od_w_out, od_ln_g, od_ln_b):
    lb_all = jnp.cumsum(jax.nn.softmax(hg_lb_logits.astype(jnp.float32), axis=0), axis=0)
    h = x
    for layer in range(DEPTH):
        j = layer // 2
        if layer % 2 == 0:
            y = even_mixer(h, ev_w_in[j], ev_conv_w[j], ev_hg_norm[j], lb_all[layer], ev_w_out[j])
            h = layer_norm(ALPHA * h + y, ev_ln_g[j], ev_ln_b[j])
        else:
            y = odd_mixer(h, od_w_in[j], od_lam_re[j], od_lam_im[j], od_log_step[j], od_b_re[j], od_b_im[j],
                          od_c_re[j], od_c_im[j], od_d[j], od_w_glu[j], od_b_glu[j], od_w_out[j])
            h = layer_norm(ALPHA * h + y, od_ln_g[j], od_ln_b[j])
    return h
```

```python
import functools
import math

import jax
import jax.numpy as jnp
from jax import lax
from jax.experimental import pallas as pl
from jax.experimental.pallas import tpu as pltpu

DEPTH = 2
ALPHA = (2 * DEPTH) ** 0.25
LN_EPS = 1e-5
RMS_EPS = 1e-6
LAMBDA_RE_MAX = -1e-4
CONV_K = 3
HG_HEADS = 16
HG_CHUNK = 32
S5_GROUP = 16
S5_STATE = 64

LANES = 128
SUBLANES = 8
VMEM_LIMIT = 56 * 1024 * 1024

MIX_TB = 256
MIX_HB = 2
S5_L = 16
S5_R = 128
S5_GB = 16

_NT = (((1,), (1,)), ((), ()))


def _bf(x):
    return x.astype(jnp.bfloat16)


def _mm_kernel(a_ref, b_ref, o_ref, a_bf):
    @pl.when(pl.program_id(1) == 0)
    def _():
        a_bf[...] = _bf(a_ref[...])

    o_ref[...] = jnp.dot(a_bf[...], b_ref[...], preferred_element_type=jnp.float32).astype(o_ref.dtype)


def _matmul(a, b, out_dtype, tm, tn):
    m, k = a.shape
    n = b.shape[1]
    return pl.pallas_call(
        _mm_kernel,
        grid=(m // tm, n // tn),
        in_specs=[pl.BlockSpec((tm, k), lambda i, j: (i, 0)),
                  pl.BlockSpec((k, tn), lambda i, j: (0, j))],
        out_specs=pl.BlockSpec((tm, tn), lambda i, j: (i, j)),
        out_shape=jax.ShapeDtypeStruct((m, n), out_dtype),
        scratch_shapes=[pltpu.VMEM((tm, k), jnp.bfloat16)],
        compiler_params=pltpu.CompilerParams(
            dimension_semantics=("parallel", "arbitrary"), vmem_limit_bytes=VMEM_LIMIT),
        name="in_proj",
    )(a, b)


def _out_ln_kernel(y_ref, w_ref, r_ref, g_ref, b_ref, o_ref):
    k = pl.program_id(1)

    @pl.when(k == 0)
    def _():
        o_ref[...] = ALPHA * r_ref[...]

    o_ref[...] += jnp.dot(y_ref[0], w_ref[...], preferred_element_type=jnp.float32)

    @pl.when(k == pl.num_programs(1) - 1)
    def _():
        z = o_ref[...]
        mu = jnp.mean(z, axis=-1, keepdims=True)
        zc = z - mu
        var = jnp.mean(zc * zc, axis=-1, keepdims=True)
        o_ref[...] = zc * lax.rsqrt(var + LN_EPS) * g_ref[...] + b_ref[...]


def _out_ln(y2, w, resid, g, b, tm, tk):
    _, t, eh = y2.shape
    e, d = w.shape
    kh = eh // tk
    return pl.pallas_call(
        _out_ln_kernel,
        grid=(t // tm, e // tk),
        in_specs=[pl.BlockSpec((1, tm, tk), lambda i, k: (k // kh, i, k % kh)),
                  pl.BlockSpec((tk, d), lambda i, k: (k, 0)),
                  pl.BlockSpec((tm, d), lambda i, k: (i, 0)),
                  pl.BlockSpec((1, d), lambda i, k: (0, 0)),
                  pl.BlockSpec((1, d), lambda i, k: (0, 0))],
        out_specs=pl.BlockSpec((tm, d), lambda i, k: (i, 0)),
        out_shape=jax.ShapeDtypeStruct((t, d), jnp.float32),
        compiler_params=pltpu.CompilerParams(
            dimension_semantics=("parallel", "arbitrary"), vmem_limit_bytes=VMEM_LIMIT),
        name="out_proj_ln",
    )(y2, w, resid, g.reshape(1, d), b.reshape(1, d))


def _chunk_cumsum(x, row_in_chunk):
    sh = 1
    while sh < HG_CHUNK:
        x = x + jnp.where(row_in_chunk >= sh, pltpu.roll(x, sh, 0), 0.0)
        sh *= 2
    return x


def _mixer0_kernel(ab_ref, ac_ref, ah_ref, q_ref, f_ref, v_ref, ga_ref, gb_ref,
                   cw_ref, lb_ref, nw_ref, o_ref, carry, state):
    tb = pl.program_id(2)
    tbk, w = ab_ref.shape
    nc = tbk // HG_CHUNK

    @pl.when(tb == 0)
    def _():
        carry[...] = jnp.zeros_like(carry)
        state[...] = jnp.zeros_like(state)

    p = ac_ref[...] * ah_ref[...]
    row = lax.broadcasted_iota(jnp.int32, (tbk, w), 0)
    c6 = carry[SUBLANES - 2:SUBLANES - 1, :]
    c7 = carry[SUBLANES - 1:SUBLANES, :]
    p1 = jnp.where(row == 0, c7, pltpu.roll(p, 1, 0))
    p2 = jnp.where(row == 0, c6, jnp.where(row == 1, c7, pltpu.roll(p, 2, 0)))
    conv = cw_ref[0:1, :] * p2 + cw_ref[1:2, :] * p1 + cw_ref[2:3, :] * p
    carry[...] = p[tbk - SUBLANES:, :]
    o_ref[0] = _bf(ab_ref[...] * conv * jax.nn.silu(ga_ref[...]))

    rowh = lax.broadcasted_iota(jnp.int32, (tbk, LANES), 0)
    ric = rowh % HG_CHUNK
    r2 = lax.broadcasted_iota(jnp.int32, (tbk, tbk), 0)
    c2 = lax.broadcasted_iota(jnp.int32, (tbk, tbk), 1)
    causal = (r2 // HG_CHUNK == c2 // HG_CHUNK) & (c2 <= r2)
    lane_chunk = lax.broadcasted_iota(jnp.int32, (LANES, tbk), 1) // HG_CHUNK
    outs = []
    for h in range(w // LANES):
        sl = slice(h * LANES, (h + 1) * LANES)
        q = q_ref[:, sl]
        v = v_ref[:, sl]
        lb = lb_ref[:, sl]
        f = lb + (1.0 - lb) * jax.nn.sigmoid(f_ref[:, sl])
        kk = 1.0 - f
        b = _chunk_cumsum(jnp.log(f), ric)
        b_last_rows = [b[(c + 1) * HG_CHUNK - 1:(c + 1) * HG_CHUNK, :] for c in range(nc)]
        b_last = jnp.concatenate([jnp.broadcast_to(r, (HG_CHUNK, LANES)) for r in b_last_rows], axis=0)
        q_in = _bf(q * jnp.exp(b))
        k_in = _bf(kk * jnp.exp(-b))
        k_dec = _bf(kk * jnp.exp(b_last - b))
        vb = _bf(v)
        s = lax.dot_general(q_in, k_in, _NT, preferred_element_type=jnp.float32)
        s = _bf(jnp.where(causal, s, 0.0))
        o_intra = jnp.dot(s, vb, preferred_element_type=jnp.float32)
        v_t = _bf(v.T)
        st = state[h]
        o_inter = []
        for c in range(nc):
            rows = slice(c * HG_CHUNK, (c + 1) * HG_CHUNK)
            o_inter.append(lax.dot_general(q_in[rows], _bf(st), _NT, preferred_element_type=jnp.float32))
            v_c = jnp.where(lane_chunk == c, v_t, jnp.zeros_like(v_t))
            upd = jnp.dot(v_c, k_dec, preferred_element_type=jnp.float32)
            st = st * jnp.exp(b_last_rows[c]) + upd
        state[h] = st
        o = o_intra + jnp.concatenate(o_inter, axis=0)
        o = o * lax.rsqrt(jnp.mean(o * o, axis=-1, keepdims=True) + RMS_EPS) * nw_ref[...]
        outs.append(o)
    ob = outs[0] if len(outs) == 1 else jnp.concatenate(outs, axis=1)
    o_ref[1] = _bf(ob * jax.nn.silu(gb_ref[...]))


def _mixer0(proj, conv_w, lb, hg_norm, bsz, seq):
    t, e_in = proj.shape
    cw = conv_w.shape[1]
    w = MIX_HB * LANES
    nsec = cw // w
    nt = seq // MIX_TB

    def sec(k):
        return pl.BlockSpec((MIX_TB, w), lambda b, g, s, k=k: (b * nt + s, k * nsec + g))

    return pl.pallas_call(
        _mixer0_kernel,
        grid=(bsz, nsec, nt),
        in_specs=[sec(0), sec(1), sec(2), sec(3), sec(4), sec(5), sec(6), sec(7),
                  pl.BlockSpec((CONV_K, w), lambda b, g, s: (0, g)),
                  pl.BlockSpec((1, w), lambda b, g, s: (0, g)),
                  pl.BlockSpec((1, LANES), lambda b, g, s: (0, 0))],
        out_specs=pl.BlockSpec((2, MIX_TB, w), lambda b, g, s: (0, b * nt + s, g)),
        out_shape=jax.ShapeDtypeStruct((2, t, cw), jnp.bfloat16),
        scratch_shapes=[pltpu.VMEM((SUBLANES, w), jnp.float32),
                        pltpu.VMEM((MIX_HB, LANES, LANES), jnp.float32)],
        compiler_params=pltpu.CompilerParams(
            dimension_semantics=("parallel", "parallel", "arbitrary"), vmem_limit_bytes=VMEM_LIMIT),
        name="mixer0",
    )(proj, proj, proj, proj, proj, proj, proj, proj, conv_w, lb.reshape(1, cw), hg_norm.reshape(1, LANES))


def _gelu_tanh(x):
    return 0.5 * x * (1.0 + jnp.tanh(math.sqrt(2.0 / math.pi) * (x + 0.044715 * (x * x * x))))


def _s5_kernel(*refs, tiles_per_seq, nh):
    u_refs = refs[:nh]
    tz_ref, bs_ref, cs_ref, lr_ref, li_ref, d_ref, o_ref = refs[nh:nh + 7]
    x_scr, y_scr, ar_scr, ai_scr, sr_scr, si_scr, st_r, st_i = refs[nh + 7:nh + 15]
    o_scrs = refs[nh + 15:]
    gb = x_scr.shape[0]
    r = S5_R
    npair = gb // 2
    gph = LANES // S5_GROUP
    p = S5_STATE

    @pl.when(pl.program_id(1) % tiles_per_seq == 0)
    def _():
        st_r[...] = jnp.zeros_like(st_r)
        st_i[...] = jnp.zeros_like(st_i)

    for j in range(S5_L):
        for h in range(nh):
            rows = u_refs[h][pl.ds(j, r, stride=S5_L), :]
            x_scr[h * gph:(h + 1) * gph, j * S5_GROUP:(j + 1) * S5_GROUP, :] = (
                _bf(rows.T).reshape(gph, S5_GROUP, r))

    for k in range(npair):
        a = []
        for g in (2 * k, 2 * k + 1):
            xg = x_scr[g]
            y_scr[g] = jnp.dot(tz_ref[g], xg, preferred_element_type=jnp.float32)
            a.append(jnp.dot(bs_ref[g], xg, preferred_element_type=jnp.float32))
        ar_scr[k * r:(k + 1) * r, :] = jnp.concatenate([a[0][:p], a[1][:p]], axis=0).T
        ai_scr[k * r:(k + 1) * r, :] = jnp.concatenate([a[0][p:], a[1][p:]], axis=0).T

    sr, si = st_r[...], st_i[...]
    lr, li = lr_ref[...], li_ref[...]
    for n in range(r):
        sr_scr[pl.ds(n, npair, stride=r), :] = sr
        si_scr[pl.ds(n, npair, stride=r), :] = si
        a_r = ar_scr[pl.ds(n, npair, stride=r), :]
        a_i = ai_scr[pl.ds(n, npair, stride=r), :]
        sr, si = sr * lr - si * li + a_r, sr * li + si * lr + a_i
    st_r[...] = sr
    st_i[...] = si

    for k in range(npair):
        srt = sr_scr[k * r:(k + 1) * r, :].T
        sit = si_scr[k * r:(k + 1) * r, :].T
        s0 = _bf(jnp.concatenate([srt[:p], sit[:p]], axis=0))
        s1 = _bf(jnp.concatenate([srt[p:], sit[p:]], axis=0))
        y_scr[2 * k] += jnp.dot(cs_ref[2 * k], s0, preferred_element_type=jnp.float32)
        y_scr[2 * k + 1] += jnp.dot(cs_ref[2 * k + 1], s1, preferred_element_type=jnp.float32)

    for j in range(S5_L):
        for h in range(nh):
            yj = y_scr[h * gph:(h + 1) * gph, j * S5_GROUP:(j + 1) * S5_GROUP, :].reshape(LANES, r).T
            uj = u_refs[h][pl.ds(j, r, stride=S5_L), :]
            dsk = d_ref[:, h * LANES:(h + 1) * LANES]
            o_scrs[h][pl.ds(j, r, stride=S5_L), :] = _gelu_tanh(yj + dsk * uj)
    for h in range(nh):
        o_ref[:, h * LANES:(h + 1) * LANES] = o_scrs[h][...]


def _s5(proj, tz, bs, cs, l16r, l16i, d_skip, seq):
    t = proj.shape[0]
    g_total = tz.shape[0]
    width = g_total * S5_GROUP
    ch = S5_GB * S5_GROUP
    nh = ch // LANES
    tile = S5_L * S5_R
    p2 = 2 * S5_STATE
    lc = S5_L * S5_GROUP
    npair = S5_GB // 2
    kern = functools.partial(_s5_kernel, tiles_per_seq=seq // tile, nh=nh)
    u_specs = [pl.BlockSpec((tile, LANES), lambda g, i, h=h: (i, g * nh + h)) for h in range(nh)]
    return pl.pallas_call(
        kern,
        grid=(g_total // S5_GB, t // tile),
        in_specs=u_specs + [
            pl.BlockSpec((S5_GB, lc, lc), lambda g, i: (g, 0, 0)),
            pl.BlockSpec((S5_GB, p2, lc), lambda g, i: (g, 0, 0)),
            pl.BlockSpec((S5_GB, lc, p2), lambda g, i: (g, 0, 0)),
            pl.BlockSpec((npair, p2), lambda g, i: (g, 0)),
            pl.BlockSpec((npair, p2), lambda g, i: (g, 0)),
            pl.BlockSpec((1, ch), lambda g, i: (0, g))],
        out_specs=pl.BlockSpec((tile, ch), lambda g, i: (i, g)),
        out_shape=jax.ShapeDtypeStruct((t, width), jnp.float32),
        scratch_shapes=[pltpu.VMEM((S5_GB, lc, S5_R), jnp.bfloat16),
                        pltpu.VMEM((S5_GB, lc, S5_R), jnp.float32),
                        pltpu.VMEM((npair * S5_R, p2), jnp.float32),
                        pltpu.VMEM((npair * S5_R, p2), jnp.float32),
                        pltpu.VMEM((npair * S5_R, p2), jnp.float32),
                        pltpu.VMEM((npair * S5_R, p2), jnp.float32),
                        pltpu.VMEM((npair, p2), jnp.float32),
                        pltpu.VMEM((npair, p2), jnp.float32)]
        + [pltpu.VMEM((tile, LANES), jnp.float32) for _ in range(nh)],
        compiler_params=pltpu.CompilerParams(
            dimension_semantics=("parallel", "arbitrary"), vmem_limit_bytes=VMEM_LIMIT),
        name="s5",
    )(*([proj] * nh), tz, bs, cs, l16r, l16i, d_skip.reshape(1, width))


def _s5_operators(lam_re, lam_im, log_step, b_re, b_im, c_re, c_im):
    f32 = jnp.float32
    hi = lax.Precision.HIGHEST
    g = lam_re.shape[0]
    lr = jnp.minimum(lam_re.astype(f32), LAMBDA_RE_MAX)
    li = lam_im.astype(f32)
    dt = jnp.exp(log_step.astype(f32))[:, None]
    tau = jnp.arange(S5_L + 1, dtype=f32)[:, None, None]
    mag = jnp.exp(tau * (lr * dt))
    ang = tau * (li * dt)
    pr, pi = mag * jnp.cos(ang), mag * jnp.sin(ang)
    nr, ni = pr[1] - 1.0, pi[1]
    den = lr * lr + li * li
    kr, ki = (nr * lr + ni * li) / den, (ni * lr - nr * li) / den
    br, bi = b_re.astype(f32), b_im.astype(f32)
    bbr = kr[..., None] * br - ki[..., None] * bi
    bbi = kr[..., None] * bi + ki[..., None] * br
    cr, ci = c_re.astype(f32), c_im.astype(f32)
    er = pr[:S5_L, :, :, None] * bbr[None] - pi[:S5_L, :, :, None] * bbi[None]
    ei = pr[:S5_L, :, :, None] * bbi[None] + pi[:S5_L, :, :, None] * bbr[None]
    kern = (jnp.einsum('gcp,tgpd->tgcd', cr, er, precision=hi)
            - jnp.einsum('gcp,tgpd->tgcd', ci, ei, precision=hi))
    jj = jnp.arange(S5_L)
    lag = jj[:, None] - jj[None, :]
    tz = jnp.where((lag >= 0)[:, :, None, None, None], kern[jnp.clip(lag, 0, S5_L - 1)], 0.0)
    tz = tz.transpose(2, 0, 3, 1, 4).reshape(g, S5_L * S5_GROUP, S5_L * S5_GROUP)
    bs_r = er[::-1].transpose(1, 2, 0, 3).reshape(g, S5_STATE, S5_L * S5_GROUP)
    bs_i = ei[::-1].transpose(1, 2, 0, 3).reshape(g, S5_STATE, S5_L * S5_GROUP)
    bs = jnp.concatenate([bs_r, bs_i], axis=1)
    cpr = cr[None] * pr[1:, :, None, :] - ci[None] * pi[1:, :, None, :]
    cpi = cr[None] * pi[1:, :, None, :] + ci[None] * pr[1:, :, None, :]
    cs = jnp.concatenate([cpr, -cpi], axis=-1).transpose(1, 0, 2, 3).reshape(g, S5_L * S5_GROUP, 2 * S5_STATE)
    l16r = pr[S5_L].reshape(g // 2, 2 * S5_STATE)
    l16i = pi[S5_L].reshape(g // 2, 2 * S5_STATE)
    return _bf(tz), _bf(bs), _bf(cs), l16r, l16i


def _glu_kernel(a_ref, w_ref, yt_ref, gt_ref, b_ref, o_ref, a_bf):
    @pl.when(pl.program_id(1) == 0)
    def _():
        a_bf[...] = _bf(a_ref[...])

    z = jnp.dot(a_bf[...], w_ref[...], preferred_element_type=jnp.float32) + b_ref[...]
    o_ref[0] = _bf(yt_ref[...] * jax.nn.sigmoid(z) * jax.nn.silu(gt_ref[...]))


def _glu(y, w, bias, proj, tm, tn):
    t, e = y.shape
    half = e // 2
    nh = half // tn
    goff = e // tn
    return pl.pallas_call(
        _glu_kernel,
        grid=(t // tm, e // tn),
        in_specs=[pl.BlockSpec((tm, e), lambda i, j: (i, 0)),
                  pl.BlockSpec((e, tn), lambda i, j: (0, j)),
                  pl.BlockSpec((tm, tn), lambda i, j: (i, j)),
                  pl.BlockSpec((tm, tn), lambda i, j: (i, goff + j)),
                  pl.BlockSpec((1, tn), lambda i, j: (0, j))],
        out_specs=pl.BlockSpec((1, tm, tn), lambda i, j: (j // nh, i, j % nh)),
        out_shape=jax.ShapeDtypeStruct((2, t, half), jnp.bfloat16),
        scratch_shapes=[pltpu.VMEM((tm, e), jnp.bfloat16)],
        compiler_params=pltpu.CompilerParams(
            dimension_semantics=("parallel", "arbitrary"), vmem_limit_bytes=VMEM_LIMIT),
        name="glu",
    )(y, w, y, proj, bias.reshape(1, e))


def kernel(x, ev_w_in, ev_conv_w, ev_hg_norm, ev_w_out, ev_ln_g, ev_ln_b, hg_lb_logits, od_w_in, od_lam_re,
           od_lam_im, od_log_step, od_b_re, od_b_im, od_c_re, od_c_im, od_d, od_w_glu, od_b_glu, od_w_out,
           od_ln_g, od_ln_b):
    bsz, seq, d = x.shape
    t = bsz * seq
    f32 = jnp.float32
    h0 = x.reshape(t, d).astype(f32)

    lb_all = jnp.cumsum(jax.nn.softmax(hg_lb_logits.astype(f32), axis=0), axis=0)

    proj0 = _matmul(h0, _bf(ev_w_in[0]), f32, tm=512, tn=1024)
    y0 = _mixer0(proj0, ev_conv_w[0].astype(f32), lb_all[0], ev_hg_norm[0].astype(f32), bsz, seq)
    h1 = _out_ln(y0, _bf(ev_w_out[0]), h0, ev_ln_g[0].astype(f32), ev_ln_b[0].astype(f32), tm=512, tk=512)

    proj1 = _matmul(h1, _bf(od_w_in[0]), f32, tm=512, tn=1024)
    tz, bs, cs, l16r, l16i = _s5_operators(od_lam_re[0], od_lam_im[0], od_log_step[0], od_b_re[0], od_b_im[0],
                                           od_c_re[0], od_c_im[0])
    ys = _s5(proj1, tz, bs, cs, l16r, l16i, od_d[0].astype(f32), seq)
    y1 = _glu(ys, _bf(od_w_glu[0]), od_b_glu[0].astype(f32), proj1, tm=512, tn=1024)
    h2 = _out_ln(y1, _bf(od_w_out[0]), h1, od_ln_g[0].astype(f32), od_ln_b[0].astype(f32), tm=512, tk=512)
    return h2.reshape(bsz, seq, d).astype(x.dtype)
```

```python
import functools
import math

import jax
import jax.numpy as jnp
from jax import lax
from jax.experimental import pallas as pl
from jax.experimental.pallas import tpu as pltpu

DEPTH = 2
ALPHA = (2 * DEPTH) ** 0.25
LN_EPS = 1e-5
RMS_EPS = 1e-6
LAMBDA_RE_MAX = -1e-4
CONV_K = 3
HG_HEADS = 16
HG_CHUNK = 32
S5_GROUP = 16
S5_STATE = 64

LANES = 128
SUBLANES = 8
VMEM_LIMIT = 60 * 1024 * 1024

MIX_TB = 256
MIX_HB = 4
S5_L = 16
S5_R = 128
S5_GB = 16

_NT = (((1,), (1,)), ((), ()))


def _bf(x):
    return x.astype(jnp.bfloat16)


def _mm_kernel(a_ref, b_ref, o_ref, a_bf):
    @pl.when(pl.program_id(1) == 0)
    def _():
        a_bf[...] = _bf(a_ref[...])

    o_ref[...] = jnp.dot(a_bf[...], b_ref[...], preferred_element_type=jnp.float32).astype(o_ref.dtype)


def _matmul(a, b, out_dtype, tm, tn):
    m, k = a.shape
    n = b.shape[1]
    return pl.pallas_call(
        _mm_kernel,
        grid=(m // tm, n // tn),
        in_specs=[pl.BlockSpec((tm, k), lambda i, j: (i, 0)),
                  pl.BlockSpec((k, tn), lambda i, j: (0, j))],
        out_specs=pl.BlockSpec((tm, tn), lambda i, j: (i, j)),
        out_shape=jax.ShapeDtypeStruct((m, n), out_dtype),
        scratch_shapes=[pltpu.VMEM((tm, k), jnp.bfloat16)],
        compiler_params=pltpu.CompilerParams(
            dimension_semantics=("parallel", "arbitrary"), vmem_limit_bytes=VMEM_LIMIT),
        name="in_proj",
    )(a, b)


def _out_ln_kernel(y_ref, w_ref, r_ref, g_ref, b_ref, o_ref, *, tn):
    eh = y_ref.shape[2]
    d = o_ref.shape[1]
    for c in range(d // tn):
        cols = slice(c * tn, (c + 1) * tn)
        acc = jnp.dot(y_ref[0], w_ref[:eh, cols], preferred_element_type=jnp.float32)
        acc += jnp.dot(y_ref[1], w_ref[eh:, cols], preferred_element_type=jnp.float32)
        o_ref[:, cols] = ALPHA * r_ref[:, cols] + acc
    z = o_ref[...]
    mu = jnp.mean(z, axis=-1, keepdims=True)
    zc = z - mu
    var = jnp.mean(zc * zc, axis=-1, keepdims=True)
    o_ref[...] = zc * lax.rsqrt(var + LN_EPS) * g_ref[...] + b_ref[...]


def _out_ln(y2, w, resid, g, b, tm, tn):
    _, t, eh = y2.shape
    e, d = w.shape
    return pl.pallas_call(
        functools.partial(_out_ln_kernel, tn=tn),
        grid=(t // tm,),
        in_specs=[pl.BlockSpec((2, tm, eh), lambda i: (0, i, 0)),
                  pl.BlockSpec((e, d), lambda i: (0, 0), pipeline_mode=pl.Buffered(1)),
                  pl.BlockSpec((tm, d), lambda i: (i, 0)),
                  pl.BlockSpec((1, d), lambda i: (0, 0)),
                  pl.BlockSpec((1, d), lambda i: (0, 0))],
        out_specs=pl.BlockSpec((tm, d), lambda i: (i, 0)),
        out_shape=jax.ShapeDtypeStruct((t, d), jnp.float32),
        compiler_params=pltpu.CompilerParams(
            dimension_semantics=("arbitrary",), vmem_limit_bytes=VMEM_LIMIT),
        name="out_proj_ln",
    )(y2, w, resid, g.reshape(1, d), b.reshape(1, d))


def _chunk_cumsum(x, row_in_chunk):
    sh = 1
    while sh < HG_CHUNK:
        x = x + jnp.where(row_in_chunk >= sh, pltpu.roll(x, sh, 0), 0.0)
        sh *= 2
    return x


def _mixer0_kernel(ab_ref, ac_ref, ah_ref, q_ref, f_ref, v_ref, ga_ref, gb_ref,
                   cw_ref, lb_ref, nw_ref, o_ref, carry, state):
    tb = pl.program_id(2)
    tbk, w = ab_ref.shape
    nc = tbk // HG_CHUNK

    @pl.when(tb == 0)
    def _():
        carry[...] = jnp.zeros_like(carry)
        state[...] = jnp.zeros_like(state)

    p = ac_ref[...] * ah_ref[...]
    row = lax.broadcasted_iota(jnp.int32, (tbk, w), 0)
    c6 = carry[SUBLANES - 2:SUBLANES - 1, :]
    c7 = carry[SUBLANES - 1:SUBLANES, :]
    p1 = jnp.where(row == 0, c7, pltpu.roll(p, 1, 0))
    p2 = jnp.where(row == 0, c6, jnp.where(row == 1, c7, pltpu.roll(p, 2, 0)))
    conv = cw_ref[0:1, :] * p2 + cw_ref[1:2, :] * p1 + cw_ref[2:3, :] * p
    carry[...] = p[tbk - SUBLANES:, :]
    o_ref[0] = _bf(ab_ref[...] * conv * jax.nn.silu(ga_ref[...]))

    rowh = lax.broadcasted_iota(jnp.int32, (tbk, LANES), 0)
    ric = rowh % HG_CHUNK
    r2 = lax.broadcasted_iota(jnp.int32, (tbk, tbk), 0)
    c2 = lax.broadcasted_iota(jnp.int32, (tbk, tbk), 1)
    causal = (r2 // HG_CHUNK == c2 // HG_CHUNK) & (c2 <= r2)
    lane_chunk = lax.broadcasted_iota(jnp.int32, (LANES, tbk), 1) // HG_CHUNK
    outs = []
    for h in range(w // LANES):
        sl = slice(h * LANES, (h + 1) * LANES)
        q = q_ref[:, sl]
        v = v_ref[:, sl]
        lb = lb_ref[:, sl]
        f = lb + (1.0 - lb) * jax.nn.sigmoid(f_ref[:, sl])
        kk = 1.0 - f
        b = _chunk_cumsum(jnp.log(f), ric)
        b_last_rows = [b[(c + 1) * HG_CHUNK - 1:(c + 1) * HG_CHUNK, :] for c in range(nc)]
        b_last = jnp.concatenate([jnp.broadcast_to(r, (HG_CHUNK, LANES)) for r in b_last_rows], axis=0)
        q_in = _bf(q * jnp.exp(b))
        k_in = _bf(kk * jnp.exp(-b))
        k_dec = _bf(kk * jnp.exp(b_last - b))
        vb = _bf(v)
        s = lax.dot_general(q_in, k_in, _NT, preferred_element_type=jnp.float32)
        s = _bf(jnp.where(causal, s, 0.0))
        o_intra = jnp.dot(s, vb, preferred_element_type=jnp.float32)
        v_t = _bf(v.T)
        zero_t = jnp.zeros_like(v_t)
        v_stack = jnp.concatenate([jnp.where(lane_chunk == c, v_t, zero_t) for c in range(nc)], axis=0)
        upd = jnp.dot(v_stack, k_dec, preferred_element_type=jnp.float32)
        st = state[h]
        sts = []
        for c in range(nc):
            sts.append(_bf(st))
            st = st * jnp.exp(b_last_rows[c]) + upd[c * LANES:(c + 1) * LANES]
        state[h] = st
        st_all = jnp.concatenate(sts, axis=1)
        zero_q = jnp.zeros_like(q_in)
        q_exp = jnp.concatenate([jnp.where(rowh // HG_CHUNK == c, q_in, zero_q) for c in range(nc)], axis=1)
        o_inter = lax.dot_general(q_exp, st_all, _NT, preferred_element_type=jnp.float32)
        o = o_intra + o_inter
        o = o * lax.rsqrt(jnp.mean(o * o, axis=-1, keepdims=True) + RMS_EPS) * nw_ref[...]
        outs.append(o)
    ob = outs[0] if len(outs) == 1 else jnp.concatenate(outs, axis=1)
    o_ref[1] = _bf(ob * jax.nn.silu(gb_ref[...]))


def _mixer0(proj, conv_w, lb, hg_norm, bsz, seq):
    t, e_in = proj.shape
    cw = conv_w.shape[1]
    w = MIX_HB * LANES
    nsec = cw // w
    nt = seq // MIX_TB

    def sec(k):
        return pl.BlockSpec((MIX_TB, w), lambda b, g, s, k=k: (b * nt + s, k * nsec + g))

    return pl.pallas_call(
        _mixer0_kernel,
        grid=(bsz, nsec, nt),
        in_specs=[sec(0), sec(1), sec(2), sec(3), sec(4), sec(5), sec(6), sec(7),
                  pl.BlockSpec((CONV_K, w), lambda b, g, s: (0, g)),
                  pl.BlockSpec((1, w), lambda b, g, s: (0, g)),
                  pl.BlockSpec((1, LANES), lambda b, g, s: (0, 0))],
        out_specs=pl.BlockSpec((2, MIX_TB, w), lambda b, g, s: (0, b * nt + s, g)),
        out_shape=jax.ShapeDtypeStruct((2, t, cw), jnp.bfloat16),
        scratch_shapes=[pltpu.VMEM((SUBLANES, w), jnp.float32),
                        pltpu.VMEM((MIX_HB, LANES, LANES), jnp.float32)],
        compiler_params=pltpu.CompilerParams(
            dimension_semantics=("parallel", "parallel", "arbitrary"), vmem_limit_bytes=VMEM_LIMIT),
        name="mixer0",
    )(proj, proj, proj, proj, proj, proj, proj, proj, conv_w, lb.reshape(1, cw), hg_norm.reshape(1, LANES))


def _gelu_tanh(x):
    return 0.5 * x * (1.0 + jnp.tanh(math.sqrt(2.0 / math.pi) * (x + 0.044715 * (x * x * x))))


def _s5_kernel(*refs, tiles_per_seq, nh):
    u_refs = refs[:nh]
    tz_ref, bs_ref, cs_ref, lr_ref, li_ref, d_ref, o_ref = refs[nh:nh + 7]
    x_scr, y_scr, ar_scr, ai_scr, sr_scr, si_scr, st_r, st_i = refs[nh + 7:nh + 15]
    o_scrs = refs[nh + 15:]
    gb = x_scr.shape[0]
    r = S5_R
    npair = gb // 2
    gph = LANES // S5_GROUP
    p = S5_STATE

    @pl.when(pl.program_id(1) % tiles_per_seq == 0)
    def _():
        st_r[...] = jnp.zeros_like(st_r)
        st_i[...] = jnp.zeros_like(st_i)

    for j in range(S5_L):
        for h in range(nh):
            rows = u_refs[h][pl.ds(j, r, stride=S5_L), :]
            x_scr[h * gph:(h + 1) * gph, j * S5_GROUP:(j + 1) * S5_GROUP, :] = (
                _bf(rows.T).reshape(gph, S5_GROUP, r))

    for k in range(npair):
        a = []
        for g in (2 * k, 2 * k + 1):
            xg = x_scr[g]
            y_scr[g] = jnp.dot(tz_ref[g], xg, preferred_element_type=jnp.float32)
            a.append(jnp.dot(bs_ref[g], xg, preferred_element_type=jnp.float32))
        ar_scr[k * r:(k + 1) * r, :] = jnp.concatenate([a[0][:p], a[1][:p]], axis=0).T
        ai_scr[k * r:(k + 1) * r, :] = jnp.concatenate([a[0][p:], a[1][p:]], axis=0).T

    sr, si = st_r[...], st_i[...]
    lr, li = lr_ref[...], li_ref[...]
    for n in range(r):
        sr_scr[pl.ds(n, npair, stride=r), :] = sr
        si_scr[pl.ds(n, npair, stride=r), :] = si
        a_r = ar_scr[pl.ds(n, npair, stride=r), :]
        a_i = ai_scr[pl.ds(n, npair, stride=r), :]
        sr, si = sr * lr - si * li + a_r, sr * li + si * lr + a_i
    st_r[...] = sr
    st_i[...] = si

    for k in range(npair):
        srt = sr_scr[k * r:(k + 1) * r, :].T
        sit = si_scr[k * r:(k + 1) * r, :].T
        s0 = _bf(jnp.concatenate([srt[:p], sit[:p]], axis=0))
        s1 = _bf(jnp.concatenate([srt[p:], sit[p:]], axis=0))
        y_scr[2 * k] += jnp.dot(cs_ref[2 * k], s0, preferred_element_type=jnp.float32)
        y_scr[2 * k + 1] += jnp.dot(cs_ref[2 * k + 1], s1, preferred_element_type=jnp.float32)

    for j in range(S5_L):
        for h in range(nh):
            yj = y_scr[h * gph:(h + 1) * gph, j * S5_GROUP:(j + 1) * S5_GROUP, :].reshape(LANES, r).T
            uj = u_refs[h][pl.ds(j, r, stride=S5_L), :]
            dsk = d_ref[:, h * LANES:(h + 1) * LANES]
            o_scrs[h][pl.ds(j, r, stride=S5_L), :] = _gelu_tanh(yj + dsk * uj)
    for h in range(nh):
        o_ref[:, h * LANES:(h + 1) * LANES] = o_scrs[h][...]


def _s5(proj, tz, bs, cs, l16r, l16i, d_skip, seq):
    t = proj.shape[0]
    g_total = tz.shape[0]
    width = g_total * S5_GROUP
    ch = S5_GB * S5_GROUP
    nh = ch // LANES
    tile = S5_L * S5_R
    p2 = 2 * S5_STATE
    lc = S5_L * S5_GROUP
    npair = S5_GB // 2
    kern = functools.partial(_s5_kernel, tiles_per_seq=seq // tile, nh=nh)
    u_specs = [pl.BlockSpec((tile, LANES), lambda g, i, h=h: (i, g * nh + h)) for h in range(nh)]
    return pl.pallas_call(
        kern,
        grid=(g_total // S5_GB, t // tile),
        in_specs=u_specs + [
            pl.BlockSpec((S5_GB, lc, lc), lambda g, i: (g, 0, 0)),
            pl.BlockSpec((S5_GB, p2, lc), lambda g, i: (g, 0, 0)),
            pl.BlockSpec((S5_GB, lc, p2), lambda g, i: (g, 0, 0)),
            pl.BlockSpec((npair, p2), lambda g, i: (g, 0)),
            pl.BlockSpec((npair, p2), lambda g, i: (g, 0)),
            pl.BlockSpec((1, ch), lambda g, i: (0, g))],
        out_specs=pl.BlockSpec((tile, ch), lambda g, i: (i, g)),
        out_shape=jax.ShapeDtypeStruct((t, width), jnp.float32),
        scratch_shapes=[pltpu.VMEM((S5_GB, lc, S5_R), jnp.bfloat16),
                        pltpu.VMEM((S5_GB, lc, S5_R), jnp.float32),
                        pltpu.VMEM((npair * S5_R, p2), jnp.float32),
                        pltpu.VMEM((npair * S5_R, p2), jnp.float32),
                        pltpu.VMEM((npair * S5_R, p2), jnp.float32),
                        pltpu.VMEM((npair * S5_R, p2), jnp.float32),
                        pltpu.VMEM((npair, p2), jnp.float32),
                        pltpu.VMEM((npair, p2), jnp.float32)]
        + [pltpu.VMEM((tile, LANES), jnp.float32) for _ in range(nh)],
        compiler_params=pltpu.CompilerParams(
            dimension_semantics=("parallel", "arbitrary"), vmem_limit_bytes=VMEM_LIMIT),
        name="s5",
    )(*([proj] * nh), tz, bs, cs, l16r, l16i, d_skip.reshape(1, width))


def _s5_operators(lam_re, lam_im, log_step, b_re, b_im, c_re, c_im):
    f32 = jnp.float32
    hi = lax.Precision.HIGHEST
    g = lam_re.shape[0]
    lr = jnp.minimum(lam_re.astype(f32), LAMBDA_RE_MAX)
    li = lam_im.astype(f32)
    dt = jnp.exp(log_step.astype(f32))[:, None]
    tau = jnp.arange(S5_L + 1, dtype=f32)[:, None, None]
    mag = jnp.exp(tau * (lr * dt))
    ang = tau * (li * dt)
    pr, pi = mag * jnp.cos(ang), mag * jnp.sin(ang)
    nr, ni = pr[1] - 1.0, pi[1]
    den = lr * lr + li * li
    kr, ki = (nr * lr + ni * li) / den, (ni * lr - nr * li) / den
    br, bi = b_re.astype(f32), b_im.astype(f32)
    bbr = kr[..., None] * br - ki[..., None] * bi
    bbi = kr[..., None] * bi + ki[..., None] * br
    cr, ci = c_re.astype(f32), c_im.astype(f32)
    er = pr[:S5_L, :, :, None] * bbr[None] - pi[:S5_L, :, :, None] * bbi[None]
    ei = pr[:S5_L, :, :, None] * bbi[None] + pi[:S5_L, :, :, None] * bbr[None]
    kern = (jnp.einsum('gcp,tgpd->tgcd', cr, er, precision=hi)
            - jnp.einsum('gcp,tgpd->tgcd', ci, ei, precision=hi))
    kpad = jnp.concatenate([kern[::-1], jnp.zeros((S5_L - 1,) + kern.shape[1:], f32)], axis=0)
    tz = jnp.stack([kpad[S5_L - 1 - j:2 * S5_L - 1 - j] for j in range(S5_L)], axis=0)
    tz = tz.transpose(2, 0, 3, 1, 4).reshape(g, S5_L * S5_GROUP, S5_L * S5_GROUP)
    bs_r = er[::-1].transpose(1, 2, 0, 3).reshape(g, S5_STATE, S5_L * S5_GROUP)
    bs_i = ei[::-1].transpose(1, 2, 0, 3).reshape(g, S5_STATE, S5_L * S5_GROUP)
    bs = jnp.concatenate([bs_r, bs_i], axis=1)
    cpr = cr[None] * pr[1:, :, None, :] - ci[None] * pi[1:, :, None, :]
    cpi = cr[None] * pi[1:, :, None, :] + ci[None] * pr[1:, :, None, :]
    cs = jnp.concatenate([cpr, -cpi], axis=-1).transpose(1, 0, 2, 3).reshape(g, S5_L * S5_GROUP, 2 * S5_STATE)
    l16r = pr[S5_L].reshape(g // 2, 2 * S5_STATE)
    l16i = pi[S5_L].reshape(g // 2, 2 * S5_STATE)
    return _bf(tz), _bf(bs), _bf(cs), l16r, l16i


def _glu_kernel(a_ref, w_ref, yt_ref, gt_ref, b_ref, o_ref, a_bf):
    @pl.when(pl.program_id(1) == 0)
    def _():
        a_bf[...] = _bf(a_ref[...])

    tn = w_ref.shape[1]
    sw = 2 * LANES
    for c in range(tn // sw):
        cols = slice(c * sw, (c + 1) * sw)
        z = jnp.dot(a_bf[...], w_ref[:, cols], preferred_element_type=jnp.float32) + b_ref[:, cols]
        o_ref[0, :, cols] = _bf(yt_ref[:, cols] * jax.nn.sigmoid(z) * jax.nn.silu(gt_ref[:, cols]))


def _glu(y, w, bias, proj, tm, tn):
    t, e = y.shape
    half = e // 2
    nh = half // tn
    goff = e // tn
    return pl.pallas_call(
        _glu_kernel,
        grid=(t // tm, e // tn),
        in_specs=[pl.BlockSpec((tm, e), lambda i, j: (i, 0)),
                  pl.BlockSpec((e, tn), lambda i, j: (0, j)),
                  pl.BlockSpec((tm, tn), lambda i, j: (i, j)),
                  pl.BlockSpec((tm, tn), lambda i, j: (i, goff + j)),
                  pl.BlockSpec((1, tn), lambda i, j: (0, j))],
        out_specs=pl.BlockSpec((1, tm, tn), lambda i, j: (j // nh, i, j % nh)),
        out_shape=jax.ShapeDtypeStruct((2, t, half), jnp.bfloat16),
        scratch_shapes=[pltpu.VMEM((tm, e), jnp.bfloat16)],
        compiler_params=pltpu.CompilerParams(
            dimension_semantics=("parallel", "arbitrary"), vmem_limit_bytes=VMEM_LIMIT),
        name="glu",
    )(y, w, y, proj, bias.reshape(1, e))


def kernel(x, ev_w_in, ev_conv_w, ev_hg_norm, ev_w_out, ev_ln_g, ev_ln_b, hg_lb_logits, od_w_in, od_lam_re,
           od_lam_im, od_log_step, od_b_re, od_b_im, od_c_re, od_c_im, od_d, od_w_glu, od_b_glu, od_w_out,
           od_ln_g, od_ln_b):
    bsz, seq, d = x.shape
    t = bsz * seq
    f32 = jnp.float32
    h0 = x.reshape(t, d).astype(f32)

    lb_all = jnp.cumsum(jax.nn.softmax(hg_lb_logits.astype(f32), axis=0), axis=0)

    proj0 = _matmul(h0, _bf(ev_w_in[0]), f32, tm=512, tn=1024)
    y0 = _mixer0(proj0, ev_conv_w[0].astype(f32), lb_all[0], ev_hg_norm[0].astype(f32), bsz, seq)
    h1 = _out_ln(y0, _bf(ev_w_out[0]), h0, ev_ln_g[0].astype(f32), ev_ln_b[0].astype(f32), tm=256, tn=1024)

    proj1 = _matmul(h1, _bf(od_w_in[0]), f32, tm=512, tn=1024)
    tz, bs, cs, l16r, l16i = _s5_operators(od_lam_re[0], od_lam_im[0], od_log_step[0], od_b_re[0], od_b_im[0],
                                           od_c_re[0], od_c_im[0])
    ys = _s5(proj1, tz, bs, cs, l16r, l16i, od_d[0].astype(f32), seq)
    y1 = _glu(ys, _bf(od_w_glu[0]), od_b_glu[0].astype(f32), proj1, tm=512, tn=1024)
    h2 = _out_ln(y1, _bf(od_w_out[0]), h1, od_ln_g[0].astype(f32), od_ln_b[0].astype(f32), tm=256, tn=1024)
    return h2.reshape(bsz, seq, d).astype(x.dtype)
```

```python
import functools
import math

import jax
import jax.numpy as jnp
from jax import lax
from jax.experimental import pallas as pl
from jax.experimental.pallas import tpu as pltpu

DEPTH = 2
ALPHA = (2 * DEPTH) ** 0.25
LN_EPS = 1e-5
RMS_EPS = 1e-6
LAMBDA_RE_MAX = -1e-4
CONV_K = 3
HG_HEADS = 16
HG_CHUNK = 32
S5_GROUP = 16
S5_STATE = 64

LANES = 128
SUBLANES = 8
VMEM_LIMIT = 60 * 1024 * 1024

MIX_TB = 256
MIX_HB = 8
S5_L = 16
S5_R = 128
S5_GB = 16

_NT = (((1,), (1,)), ((), ()))


def _bf(x):
    return x.astype(jnp.bfloat16)


def _mm_kernel(a_ref, b_ref, o_ref, a_bf):
    @pl.when(pl.program_id(1) == 0)
    def _():
        a_bf[...] = _bf(a_ref[...])

    o_ref[...] = jnp.dot(a_bf[...], b_ref[...], preferred_element_type=jnp.float32).astype(o_ref.dtype)


def _matmul(a, b, out_dtype, tm, tn):
    m, k = a.shape
    n = b.shape[1]
    return pl.pallas_call(
        _mm_kernel,
        grid=(m // tm, n // tn),
        in_specs=[pl.BlockSpec((tm, k), lambda i, j: (i, 0)),
                  pl.BlockSpec((k, tn), lambda i, j: (0, j))],
        out_specs=pl.BlockSpec((tm, tn), lambda i, j: (i, j)),
        out_shape=jax.ShapeDtypeStruct((m, n), out_dtype),
        scratch_shapes=[pltpu.VMEM((tm, k), jnp.bfloat16)],
        compiler_params=pltpu.CompilerParams(
            dimension_semantics=("parallel", "arbitrary"), vmem_limit_bytes=VMEM_LIMIT),
        name="in_proj",
    )(a, b)


def _out_ln_kernel(y_ref, w_ref, r_ref, g_ref, b_ref, o_ref, *, tn):
    eh = y_ref.shape[2]
    tm, d = o_ref.shape
    for c in range(d // tn):
        cols = slice(c * tn, (c + 1) * tn)
        acc = jnp.dot(y_ref[0], w_ref[:eh, cols], preferred_element_type=jnp.float32)
        acc += jnp.dot(y_ref[1], w_ref[eh:, cols], preferred_element_type=jnp.float32)
        o_ref[:, cols] = ALPHA * r_ref[:, cols] + acc
    gam, bet = g_ref[...], b_ref[...]
    for rb in range(tm // SUBLANES):
        rows = slice(rb * SUBLANES, (rb + 1) * SUBLANES)
        z = o_ref[rows, :]
        mu = jnp.mean(z, axis=-1, keepdims=True)
        zc = z - mu
        var = jnp.mean(zc * zc, axis=-1, keepdims=True)
        o_ref[rows, :] = zc * lax.rsqrt(var + LN_EPS) * gam + bet


def _out_ln(y2, w, resid, g, b, tm, tn):
    _, t, eh = y2.shape
    e, d = w.shape
    return pl.pallas_call(
        functools.partial(_out_ln_kernel, tn=tn),
        grid=(t // tm,),
        in_specs=[pl.BlockSpec((2, tm, eh), lambda i: (0, i, 0)),
                  pl.BlockSpec((e, d), lambda i: (0, 0), pipeline_mode=pl.Buffered(1)),
                  pl.BlockSpec((tm, d), lambda i: (i, 0)),
                  pl.BlockSpec((1, d), lambda i: (0, 0)),
                  pl.BlockSpec((1, d), lambda i: (0, 0))],
        out_specs=pl.BlockSpec((tm, d), lambda i: (i, 0)),
        out_shape=jax.ShapeDtypeStruct((t, d), jnp.float32),
        compiler_params=pltpu.CompilerParams(
            dimension_semantics=("arbitrary",), vmem_limit_bytes=VMEM_LIMIT),
        name="out_proj_ln",
    )(y2, w, resid, g.reshape(1, d), b.reshape(1, d))


def _chunk_cumsum(x, row_in_chunk):
    sh = 1
    while sh < HG_CHUNK:
        x = x + jnp.where(row_in_chunk >= sh, pltpu.roll(x, sh, 0), 0.0)
        sh *= 2
    return x


def _mixer0_kernel(ab_ref, ac_ref, ah_ref, q_ref, f_ref, v_ref, ga_ref, gb_ref,
                   cw_ref, lb_ref, nw_ref, o_ref, carry, state):
    tb = pl.program_id(2)
    tbk, w = ab_ref.shape
    nc = tbk // HG_CHUNK

    @pl.when(tb == 0)
    def _():
        carry[...] = jnp.zeros_like(carry)
        state[...] = jnp.zeros_like(state)

    p = ac_ref[...] * ah_ref[...]
    row = lax.broadcasted_iota(jnp.int32, (tbk, w), 0)
    c6 = carry[SUBLANES - 2:SUBLANES - 1, :]
    c7 = carry[SUBLANES - 1:SUBLANES, :]
    p1 = jnp.where(row == 0, c7, pltpu.roll(p, 1, 0))
    p2 = jnp.where(row == 0, c6, jnp.where(row == 1, c7, pltpu.roll(p, 2, 0)))
    conv = cw_ref[0:1, :] * p2 + cw_ref[1:2, :] * p1 + cw_ref[2:3, :] * p
    carry[...] = p[tbk - SUBLANES:, :]
    o_ref[0] = _bf(ab_ref[...] * conv * jax.nn.silu(ga_ref[...]))

    rowh = lax.broadcasted_iota(jnp.int32, (tbk, LANES), 0)
    ric = rowh % HG_CHUNK
    r2 = lax.broadcasted_iota(jnp.int32, (tbk, tbk), 0)
    c2 = lax.broadcasted_iota(jnp.int32, (tbk, tbk), 1)
    causal = (r2 // HG_CHUNK == c2 // HG_CHUNK) & (c2 <= r2)
    lane_chunk = lax.broadcasted_iota(jnp.int32, (LANES, tbk), 1) // HG_CHUNK
    outs = []
    for h in range(w // LANES):
        sl = slice(h * LANES, (h + 1) * LANES)
        q = q_ref[:, sl]
        v = v_ref[:, sl]
        lb = lb_ref[:, sl]
        f = lb + (1.0 - lb) * jax.nn.sigmoid(f_ref[:, sl])
        kk = 1.0 - f
        b = _chunk_cumsum(jnp.log(f), ric)
        b_last_rows = [b[(c + 1) * HG_CHUNK - 1:(c + 1) * HG_CHUNK, :] for c in range(nc)]
        b_last = jnp.concatenate([jnp.broadcast_to(r, (HG_CHUNK, LANES)) for r in b_last_rows], axis=0)
        q_in = _bf(q * jnp.exp(b))
        k_in = _bf(kk * jnp.exp(-b))
        k_dec = _bf(kk * jnp.exp(b_last - b))
        vb = _bf(v)
        s = lax.dot_general(q_in, k_in, _NT, preferred_element_type=jnp.float32)
        s = _bf(jnp.where(causal, s, 0.0))
        o_intra = jnp.dot(s, vb, preferred_element_type=jnp.float32)
        v_t = _bf(v.T)
        zero_t = jnp.zeros_like(v_t)
        v_stack = jnp.concatenate([jnp.where(lane_chunk == c, v_t, zero_t) for c in range(nc)], axis=0)
        upd = jnp.dot(v_stack, k_dec, preferred_element_type=jnp.float32)
        st = state[h]
        sts = []
        for c in range(nc):
            sts.append(_bf(st))
            st = st * jnp.exp(b_last_rows[c]) + upd[c * LANES:(c + 1) * LANES]
        state[h] = st
        st_all = jnp.concatenate(sts, axis=1)
        zero_q = jnp.zeros_like(q_in)
        q_exp = jnp.concatenate([jnp.where(rowh // HG_CHUNK == c, q_in, zero_q) for c in range(nc)], axis=1)
        o_inter = lax.dot_general(q_exp, st_all, _NT, preferred_element_type=jnp.float32)
        o = o_intra + o_inter
        o = o * lax.rsqrt(jnp.mean(o * o, axis=-1, keepdims=True) + RMS_EPS) * nw_ref[...]
        outs.append(o)
    ob = outs[0] if len(outs) == 1 else jnp.concatenate(outs, axis=1)
    o_ref[1] = _bf(ob * jax.nn.silu(gb_ref[...]))


def _mixer0(proj, conv_w, lb, hg_norm, bsz, seq):
    t, e_in = proj.shape
    cw = conv_w.shape[1]
    w = MIX_HB * LANES
    nsec = cw // w
    nt = seq // MIX_TB

    def sec(k):
        return pl.BlockSpec((MIX_TB, w), lambda b, g, s, k=k: (b * nt + s, k * nsec + g))

    return pl.pallas_call(
        _mixer0_kernel,
        grid=(bsz, nsec, nt),
        in_specs=[sec(0), sec(1), sec(2), sec(3), sec(4), sec(5), sec(6), sec(7),
                  pl.BlockSpec((CONV_K, w), lambda b, g, s: (0, g)),
                  pl.BlockSpec((1, w), lambda b, g, s: (0, g)),
                  pl.BlockSpec((1, LANES), lambda b, g, s: (0, 0))],
        out_specs=pl.BlockSpec((2, MIX_TB, w), lambda b, g, s: (0, b * nt + s, g)),
        out_shape=jax.ShapeDtypeStruct((2, t, cw), jnp.bfloat16),
        scratch_shapes=[pltpu.VMEM((SUBLANES, w), jnp.float32),
                        pltpu.VMEM((MIX_HB, LANES, LANES), jnp.float32)],
        compiler_params=pltpu.CompilerParams(
            dimension_semantics=("parallel", "parallel", "arbitrary"), vmem_limit_bytes=VMEM_LIMIT),
        name="mixer0",
    )(proj, proj, proj, proj, proj, proj, proj, proj, conv_w, lb.reshape(1, cw), hg_norm.reshape(1, LANES))


def _gelu_tanh(x):
    return 0.5 * x * (1.0 + jnp.tanh(math.sqrt(2.0 / math.pi) * (x + 0.044715 * (x * x * x))))


def _s5_kernel(*refs, nseq, nh):
    u_refs = refs[:nseq * nh]
    tz_ref, bs_ref, cs_ref, lr_ref, li_ref, d_ref, o_ref = refs[nseq * nh:nseq * nh + 7]
    scr = refs[nseq * nh + 7:]
    x_scr, y_scr, st_r, st_i = scr[:4]
    ar_scrs, ai_scrs, sr_scrs, si_scrs = (scr[4 + q * nseq:4 + (q + 1) * nseq] for q in range(4))
    o_scrs = scr[4 + 4 * nseq:]
    gb = tz_ref.shape[0]
    r = S5_R
    npair = gb // 2
    gph = LANES // S5_GROUP
    p = S5_STATE

    @pl.when(pl.program_id(1) == 0)
    def _():
        st_r[...] = jnp.zeros_like(st_r)
        st_i[...] = jnp.zeros_like(st_i)

    for b in range(nseq):
        for j in range(S5_L):
            for h in range(nh):
                rows = u_refs[b * nh + h][pl.ds(j, r, stride=S5_L), :]
                x_scr[b * gb + h * gph:b * gb + (h + 1) * gph, j * S5_GROUP:(j + 1) * S5_GROUP, :] = (
                    _bf(rows.T).reshape(gph, S5_GROUP, r))

    for b in range(nseq):
        for k in range(npair):
            a = []
            for g in (2 * k, 2 * k + 1):
                xg = x_scr[b * gb + g]
                y_scr[b * gb + g] = jnp.dot(tz_ref[g], xg, preferred_element_type=jnp.float32)
                a.append(jnp.dot(bs_ref[g], xg, preferred_element_type=jnp.float32))
            ar_scrs[b][k * r:(k + 1) * r, :] = jnp.concatenate([a[0][:p], a[1][:p]], axis=0).T
            ai_scrs[b][k * r:(k + 1) * r, :] = jnp.concatenate([a[0][p:], a[1][p:]], axis=0).T

    lr, li = lr_ref[...], li_ref[...]
    srs = [st_r[b] for b in range(nseq)]
    sis = [st_i[b] for b in range(nseq)]
    for n in range(r):
        for b in range(nseq):
            sr, si = srs[b], sis[b]
            sr_scrs[b][pl.ds(n, npair, stride=r), :] = sr
            si_scrs[b][pl.ds(n, npair, stride=r), :] = si
            a_r = ar_scrs[b][pl.ds(n, npair, stride=r), :]
            a_i = ai_scrs[b][pl.ds(n, npair, stride=r), :]
            srs[b], sis[b] = sr * lr - si * li + a_r, sr * li + si * lr + a_i
    for b in range(nseq):
        st_r[b] = srs[b]
        st_i[b] = sis[b]

    for b in range(nseq):
        for k in range(npair):
            srt = sr_scrs[b][k * r:(k + 1) * r, :].T
            sit = si_scrs[b][k * r:(k + 1) * r, :].T
            s0 = _bf(jnp.concatenate([srt[:p], sit[:p]], axis=0))
            s1 = _bf(jnp.concatenate([srt[p:], sit[p:]], axis=0))
            y_scr[b * gb + 2 * k] += jnp.dot(cs_ref[2 * k], s0, preferred_element_type=jnp.float32)
            y_scr[b * gb + 2 * k + 1] += jnp.dot(cs_ref[2 * k + 1], s1, preferred_element_type=jnp.float32)

    for b in range(nseq):
        for j in range(S5_L):
            for h in range(nh):
                yj = y_scr[b * gb + h * gph:b * gb + (h + 1) * gph, j * S5_GROUP:(j + 1) * S5_GROUP, :]
                yj = yj.reshape(LANES, r).T
                uj = u_refs[b * nh + h][pl.ds(j, r, stride=S5_L), :]
                dsk = d_ref[:, h * LANES:(h + 1) * LANES]
                o_scrs[b * nh + h][pl.ds(j, r, stride=S5_L), :] = _gelu_tanh(yj + dsk * uj)
        for h in range(nh):
            o_ref[b, :, h * LANES:(h + 1) * LANES] = o_scrs[b * nh + h][...]


def _s5(proj, tz, bs, cs, l16r, l16i, d_skip, bsz, seq):
    g_total = tz.shape[0]
    width = g_total * S5_GROUP
    ch = S5_GB * S5_GROUP
    nh = ch // LANES
    tile = S5_L * S5_R
    nt = seq // tile
    p2 = 2 * S5_STATE
    lc = S5_L * S5_GROUP
    npair = S5_GB // 2
    kern = functools.partial(_s5_kernel, nseq=bsz, nh=nh)
    u_specs = [pl.BlockSpec((tile, LANES), lambda g, i, b=b, h=h: (b * nt + i, g * nh + h))
               for b in range(bsz) for h in range(nh)]
    state_rows = pltpu.VMEM((npair * S5_R, p2), jnp.float32)
    out = pl.pallas_call(
        kern,
        grid=(g_total // S5_GB, nt),
        in_specs=u_specs + [
            pl.BlockSpec((S5_GB, lc, lc), lambda g, i: (g, 0, 0)),
            pl.BlockSpec((S5_GB, p2, lc), lambda g, i: (g, 0, 0)),
            pl.BlockSpec((S5_GB, lc, p2), lambda g, i: (g, 0, 0)),
            pl.BlockSpec((npair, p2), lambda g, i: (g, 0)),
            pl.BlockSpec((npair, p2), lambda g, i: (g, 0)),
            pl.BlockSpec((1, ch), lambda g, i: (0, g))],
        out_specs=pl.BlockSpec((bsz, tile, ch), lambda g, i: (0, i, g)),
        out_shape=jax.ShapeDtypeStruct((bsz, seq, width), jnp.float32),
        scratch_shapes=[pltpu.VMEM((bsz * S5_GB, lc, S5_R), jnp.bfloat16),
                        pltpu.VMEM((bsz * S5_GB, lc, S5_R), jnp.float32),
                        pltpu.VMEM((bsz, npair, p2), jnp.float32),
                        pltpu.VMEM((bsz, npair, p2), jnp.float32)]
        + [state_rows for _ in range(4 * bsz)]
        + [pltpu.VMEM((tile, LANES), jnp.float32) for _ in range(bsz * nh)],
        compiler_params=pltpu.CompilerParams(
            dimension_semantics=("parallel", "arbitrary"), vmem_limit_bytes=VMEM_LIMIT),
        name="s5",
    )(*([proj] * (bsz * nh)), tz, bs, cs, l16r, l16i, d_skip.reshape(1, width))
    return out.reshape(bsz * seq, width)


def _s5_operators(lam_re, lam_im, log_step, b_re, b_im, c_re, c_im):
    f32 = jnp.float32
    hi = lax.Precision.HIGHEST
    g = lam_re.shape[0]
    lr = jnp.minimum(lam_re.astype(f32), LAMBDA_RE_MAX)
    li = lam_im.astype(f32)
    dt = jnp.exp(log_step.astype(f32))[:, None]
    nl, nc, npz = S5_L, S5_GROUP, S5_STATE
    lc = nl * nc
    a, w = lr * dt, li * dt

    def lam_pow(a_, w_, tau):
        mag = jnp.exp(a_ * tau)
        return mag * jnp.cos(w_ * tau), mag * jnp.sin(w_ * tau)

    p1r, p1i = lam_pow(a, w, 1.0)
    nr, ni = p1r - 1.0, p1i
    den = lr * lr + li * li
    kr, ki = (nr * lr + ni * li) / den, (ni * lr - nr * li) / den
    br, bi = b_re.astype(f32), b_im.astype(f32)
    bbr = kr[..., None] * br - ki[..., None] * bi
    bbi = kr[..., None] * bi + ki[..., None] * br
    cr, ci = c_re.astype(f32), c_im.astype(f32)
    per, pei = lam_pow(a[..., None], w[..., None], jnp.arange(nl, dtype=f32))
    er = per[..., None] * bbr[:, :, None, :] - pei[..., None] * bbi[:, :, None, :]
    ei = per[..., None] * bbi[:, :, None, :] + pei[..., None] * bbr[:, :, None, :]
    kcat = (jnp.einsum('gcp,gpx->gcx', cr, er.reshape(g, npz, lc), precision=hi)
            - jnp.einsum('gcp,gpx->gcx', ci, ei.reshape(g, npz, lc), precision=hi))
    krev = kcat.reshape(g, nc, nl, nc)[:, :, ::-1, :].reshape(g, nc, lc)
    kpad = jnp.concatenate([krev, jnp.zeros((g, nc, lc - nc), f32)], axis=-1)
    tz = jnp.stack([kpad[:, :, (nl - 1 - j) * nc:(2 * nl - 1 - j) * nc] for j in range(nl)], axis=1)
    tz = tz.reshape(g, lc, lc)
    bs = jnp.concatenate([er[:, :, ::-1, :].reshape(g, npz, lc), ei[:, :, ::-1, :].reshape(g, npz, lc)], axis=1)
    pcr, pci = lam_pow(a[:, None, :], w[:, None, :], jnp.arange(1, nl + 1, dtype=f32)[:, None])
    cpr = cr[:, None] * pcr[:, :, None, :] - ci[:, None] * pci[:, :, None, :]
    cpi = cr[:, None] * pci[:, :, None, :] + ci[:, None] * pcr[:, :, None, :]
    cs = jnp.concatenate([cpr, -cpi], axis=-1).reshape(g, lc, 2 * npz)
    plr, pli = lam_pow(a, w, float(nl))
    l16r = plr.reshape(g // 2, 2 * npz)
    l16i = pli.reshape(g // 2, 2 * npz)
    return _bf(tz), _bf(bs), _bf(cs), l16r, l16i


def _sigmoid_tanh(x):
    return 0.5 + 0.5 * jnp.tanh(0.5 * x)


def _glu_kernel(a_ref, w_ref, yt_ref, gt_ref, b_ref, o_ref, a_bf, z_scr):
    @pl.when(pl.program_id(1) == 0)
    def _():
        a_bf[...] = _bf(a_ref[...])

    hm = a_bf.shape[0] // 2
    for r0 in (0, hm):
        rows = slice(r0, r0 + hm)
        z_scr[rows, :] = jnp.dot(a_bf[rows, :], w_ref[...], preferred_element_type=jnp.float32) + b_ref[...]
        g = gt_ref[rows, :]
        gate = _sigmoid_tanh(z_scr[rows, :]) * (g * _sigmoid_tanh(g))
        o_ref[0, rows, :] = _bf(yt_ref[rows, :] * gate)


def _glu(y, w, bias, proj, tm, tn):
    t, e = y.shape
    half = e // 2
    nh = half // tn
    goff = e // tn
    return pl.pallas_call(
        _glu_kernel,
        grid=(t // tm, e // tn),
        in_specs=[pl.BlockSpec((tm, e), lambda i, j: (i, 0)),
                  pl.BlockSpec((e, tn), lambda i, j: (0, j)),
                  pl.BlockSpec((tm, tn), lambda i, j: (i, j)),
                  pl.BlockSpec((tm, tn), lambda i, j: (i, goff + j)),
                  pl.BlockSpec((1, tn), lambda i, j: (0, j))],
        out_specs=pl.BlockSpec((1, tm, tn), lambda i, j: (j // nh, i, j % nh)),
        out_shape=jax.ShapeDtypeStruct((2, t, half), jnp.bfloat16),
        scratch_shapes=[pltpu.VMEM((tm, e), jnp.bfloat16), pltpu.VMEM((tm, tn), jnp.float32)],
        compiler_params=pltpu.CompilerParams(
            dimension_semantics=("parallel", "arbitrary"), vmem_limit_bytes=VMEM_LIMIT),
        name="glu",
    )(y, w, y, proj, bias.reshape(1, e))


def kernel(x, ev_w_in, ev_conv_w, ev_hg_norm, ev_w_out, ev_ln_g, ev_ln_b, hg_lb_logits, od_w_in, od_lam_re,
           od_lam_im, od_log_step, od_b_re, od_b_im, od_c_re, od_c_im, od_d, od_w_glu, od_b_glu, od_w_out,
           od_ln_g, od_ln_b):
    bsz, seq, d = x.shape
    t = bsz * seq
    f32 = jnp.float32
    h0 = x.reshape(t, d).astype(f32)

    lb_all = jnp.cumsum(jax.nn.softmax(hg_lb_logits.astype(f32), axis=0), axis=0)

    proj0 = _matmul(h0, _bf(ev_w_in[0]), f32, tm=512, tn=1024)
    y0 = _mixer0(proj0, ev_conv_w[0].astype(f32), lb_all[0], ev_hg_norm[0].astype(f32), bsz, seq)
    h1 = _out_ln(y0, _bf(ev_w_out[0]), h0, ev_ln_g[0].astype(f32), ev_ln_b[0].astype(f32), tm=256, tn=512)

    proj1 = _matmul(h1, _bf(od_w_in[0]), f32, tm=512, tn=1024)
    tz, bs, cs, l16r, l16i = _s5_operators(od_lam_re[0], od_lam_im[0], od_log_step[0], od_b_re[0], od_b_im[0],
                                           od_c_re[0], od_c_im[0])
    ys = _s5(proj1, tz, bs, cs, l16r, l16i, od_d[0].astype(f32), bsz, seq)
    y1 = _glu(ys, _bf(od_w_glu[0]), od_b_glu[0].astype(f32), proj1, tm=512, tn=1024)
    h2 = _out_ln(y1, _bf(od_w_out[0]), h1, od_ln_g[0].astype(f32), od_ln_b[0].astype(f32), tm=256, tn=512)
    return h2.reshape(bsz, seq, d).astype(x.dtype)
```

```python
import functools
import math

import jax
import jax.numpy as jnp
from jax import lax
from jax.experimental import pallas as pl
from jax.experimental.pallas import tpu as pltpu

DEPTH = 2
ALPHA = (2 * DEPTH) ** 0.25
LN_EPS = 1e-5
RMS_EPS = 1e-6
LAMBDA_RE_MAX = -1e-4
CONV_K = 3
HG_HEADS = 16
HG_CHUNK = 32
S5_GROUP = 16
S5_STATE = 64

LANES = 128
SUBLANES = 8
VMEM_LIMIT = 61 * 1024 * 1024

MIX_TB = 256
MIX_HB = 8
S5_L = 16
S5_R = 128
S5_GB = 16

_NT = (((1,), (1,)), ((), ()))


def _bf(x):
    return x.astype(jnp.bfloat16)


def _mm_kernel(a_ref, b_ref, o_ref, *maybe_a_bf):
    if maybe_a_bf:
        a_bf, = maybe_a_bf

        @pl.when(pl.program_id(1) == 0)
        def _():
            a_bf[...] = _bf(a_ref[...])
    else:
        a_bf = a_ref
    o_ref[...] = jnp.dot(a_bf[...], b_ref[...], preferred_element_type=jnp.float32).astype(o_ref.dtype)


def _matmul(a, b, out_dtype, tm, tn):
    m, k = a.shape
    n = b.shape[1]
    scratch = [] if a.dtype == jnp.bfloat16 else [pltpu.VMEM((tm, k), jnp.bfloat16)]
    return pl.pallas_call(
        _mm_kernel,
        grid=(m // tm, n // tn),
        in_specs=[pl.BlockSpec((tm, k), lambda i, j: (i, 0)),
                  pl.BlockSpec((k, tn), lambda i, j: (0, j))],
        out_specs=pl.BlockSpec((tm, tn), lambda i, j: (i, j)),
        out_shape=jax.ShapeDtypeStruct((m, n), out_dtype),
        scratch_shapes=scratch,
        compiler_params=pltpu.CompilerParams(
            dimension_semantics=("parallel", "arbitrary"), vmem_limit_bytes=VMEM_LIMIT),
        name="in_proj",
    )(a, b)


LN_ROWS = 4 * SUBLANES


def _out_ln_kernel(y_ref, w_ref, r_ref, g_ref, b_ref, o_ref, *maybe_obf, tn):
    eh = y_ref.shape[2]
    tm, d = o_ref.shape
    for c in range(d // tn):
        cols = slice(c * tn, (c + 1) * tn)
        acc = jnp.dot(y_ref[0], w_ref[:eh, cols], preferred_element_type=jnp.float32)
        acc += jnp.dot(y_ref[1], w_ref[eh:, cols], preferred_element_type=jnp.float32)
        o_ref[:, cols] = ALPHA * r_ref[:, cols] + acc

    def ln_slab(r0):
        for rb in range(LN_ROWS // SUBLANES):
            rows = slice(r0 + rb * SUBLANES, r0 + (rb + 1) * SUBLANES)
            z = o_ref[rows, :]
            mu = jnp.mean(z, axis=-1, keepdims=True)
            zc = z - mu
            var = jnp.mean(zc * zc, axis=-1, keepdims=True)
            o_ref[rows, :] = zc * lax.rsqrt(var + LN_EPS) * g_ref[...] + b_ref[...]
        if maybe_obf:
            maybe_obf[0][r0:r0 + LN_ROWS, :] = _bf(o_ref[r0:r0 + LN_ROWS, :])

    for r0 in range(0, tm, LN_ROWS):
        pl.when(pl.program_id(0) >= 0)(functools.partial(ln_slab, r0))


def _out_ln(y2, w, resid, g, b, tm, tn, with_bf16_copy):
    _, t, eh = y2.shape
    e, d = w.shape
    row_spec = pl.BlockSpec((tm, d), lambda i: (i, 0))
    out_shape = [jax.ShapeDtypeStruct((t, d), jnp.float32)]
    if with_bf16_copy:
        out_shape.append(jax.ShapeDtypeStruct((t, d), jnp.bfloat16))
    return pl.pallas_call(
        functools.partial(_out_ln_kernel, tn=tn),
        grid=(t // tm,),
        in_specs=[pl.BlockSpec((2, tm, eh), lambda i: (0, i, 0)),
                  pl.BlockSpec((e, d), lambda i: (0, 0), pipeline_mode=pl.Buffered(1)),
                  row_spec,
                  pl.BlockSpec((1, d), lambda i: (0, 0)),
                  pl.BlockSpec((1, d), lambda i: (0, 0))],
        out_specs=[row_spec] * len(out_shape),
        out_shape=out_shape,
        compiler_params=pltpu.CompilerParams(
            dimension_semantics=("arbitrary",), vmem_limit_bytes=VMEM_LIMIT),
        name="out_proj_ln",
    )(y2, w, resid, g.reshape(1, d), b.reshape(1, d))


def _chunk_cumsum(x, row_in_chunk):
    sh = 1
    while sh < HG_CHUNK:
        x = x + jnp.where(row_in_chunk >= sh, pltpu.roll(x, sh, 0), 0.0)
        sh *= 2
    return x


def _mixer0_kernel(ab_ref, ac_ref, ah_ref, q_ref, f_ref, v_ref, ga_ref, gb_ref,
                   cw_ref, lb_ref, nw_ref, o_ref, carry, state):
    tb = pl.program_id(2)
    tbk, w = ab_ref.shape
    nc = tbk // HG_CHUNK

    @pl.when(tb == 0)
    def _():
        carry[...] = jnp.zeros_like(carry)
        state[...] = jnp.zeros_like(state)

    p = ac_ref[...] * ah_ref[...]
    row = lax.broadcasted_iota(jnp.int32, (tbk, w), 0)
    c6 = carry[SUBLANES - 2:SUBLANES - 1, :]
    c7 = carry[SUBLANES - 1:SUBLANES, :]
    p1 = jnp.where(row == 0, c7, pltpu.roll(p, 1, 0))
    p2 = jnp.where(row == 0, c6, jnp.where(row == 1, c7, pltpu.roll(p, 2, 0)))
    conv = cw_ref[0:1, :] * p2 + cw_ref[1:2, :] * p1 + cw_ref[2:3, :] * p
    carry[...] = p[tbk - SUBLANES:, :]
    o_ref[0] = _bf(ab_ref[...] * conv * jax.nn.silu(ga_ref[...]))

    rowh = lax.broadcasted_iota(jnp.int32, (tbk, LANES), 0)
    ric = rowh % HG_CHUNK
    r2 = lax.broadcasted_iota(jnp.int32, (tbk, tbk), 0)
    c2 = lax.broadcasted_iota(jnp.int32, (tbk, tbk), 1)
    causal = (r2 // HG_CHUNK == c2 // HG_CHUNK) & (c2 <= r2)
    lane_chunk = lax.broadcasted_iota(jnp.int32, (LANES, tbk), 1) // HG_CHUNK
    outs = []
    for h in range(w // LANES):
        sl = slice(h * LANES, (h + 1) * LANES)
        q = q_ref[:, sl]
        v = v_ref[:, sl]
        lb = lb_ref[:, sl]
        f = lb + (1.0 - lb) * jax.nn.sigmoid(f_ref[:, sl])
        kk = 1.0 - f
        b = _chunk_cumsum(jnp.log(f), ric)
        b_last_rows = [b[(c + 1) * HG_CHUNK - 1:(c + 1) * HG_CHUNK, :] for c in range(nc)]
        b_last = jnp.concatenate([jnp.broadcast_to(r, (HG_CHUNK, LANES)) for r in b_last_rows], axis=0)
        q_in = _bf(q * jnp.exp(b))
        k_in = _bf(kk * jnp.exp(-b))
        k_dec = _bf(kk * jnp.exp(b_last - b))
        vb = _bf(v)
        s = lax.dot_general(q_in, k_in, _NT, preferred_element_type=jnp.float32)
        s = _bf(jnp.where(causal, s, 0.0))
        o_intra = jnp.dot(s, vb, preferred_element_type=jnp.float32)
        v_t = _bf(v.T)
        zero_t = jnp.zeros_like(v_t)
        v_stack = jnp.concatenate([jnp.where(lane_chunk == c, v_t, zero_t) for c in range(nc)], axis=0)
        upd = jnp.dot(v_stack, k_dec, preferred_element_type=jnp.float32)
        st = state[h]
        sts = []
        for c in range(nc):
            sts.append(_bf(st))
            st = st * jnp.exp(b_last_rows[c]) + upd[c * LANES:(c + 1) * LANES]
        state[h] = st
        st_all = jnp.concatenate(sts, axis=1)
        zero_q = jnp.zeros_like(q_in)
        q_exp = jnp.concatenate([jnp.where(rowh // HG_CHUNK == c, q_in, zero_q) for c in range(nc)], axis=1)
        o_inter = lax.dot_general(q_exp, st_all, _NT, preferred_element_type=jnp.float32)
        o = o_intra + o_inter
        o = o * lax.rsqrt(jnp.mean(o * o, axis=-1, keepdims=True) + RMS_EPS) * nw_ref[...]
        outs.append(o)
    ob = outs[0] if len(outs) == 1 else jnp.concatenate(outs, axis=1)
    o_ref[1] = _bf(ob * jax.nn.silu(gb_ref[...]))


def _mixer0(proj, conv_w, lb, hg_norm, bsz, seq):
    t, e_in = proj.shape
    cw = conv_w.shape[1]
    w = MIX_HB * LANES
    nsec = cw // w
    nt = seq // MIX_TB

    def sec(k):
        return pl.BlockSpec((MIX_TB, w), lambda b, g, s, k=k: (b * nt + s, k * nsec + g))

    return pl.pallas_call(
        _mixer0_kernel,
        grid=(bsz, nsec, nt),
        in_specs=[sec(0), sec(1), sec(2), sec(3), sec(4), sec(5), sec(6), sec(7),
                  pl.BlockSpec((CONV_K, w), lambda b, g, s: (0, g)),
                  pl.BlockSpec((1, w), lambda b, g, s: (0, g)),
                  pl.BlockSpec((1, LANES), lambda b, g, s: (0, 0))],
        out_specs=pl.BlockSpec((2, MIX_TB, w), lambda b, g, s: (0, b * nt + s, g)),
        out_shape=jax.ShapeDtypeStruct((2, t, cw), jnp.bfloat16),
        scratch_shapes=[pltpu.VMEM((SUBLANES, w), jnp.float32),
                        pltpu.VMEM((MIX_HB, LANES, LANES), jnp.float32)],
        compiler_params=pltpu.CompilerParams(
            dimension_semantics=("parallel", "parallel", "arbitrary"), vmem_limit_bytes=VMEM_LIMIT),
        name="mixer0",
    )(proj, proj, proj, proj, proj, proj, proj, proj, conv_w, lb.reshape(1, cw), hg_norm.reshape(1, LANES))


def _gelu_tanh(x):
    return 0.5 * x * (1.0 + jnp.tanh(math.sqrt(2.0 / math.pi) * (x + 0.044715 * (x * x * x))))


def _s5_kernel(*refs, nseq, nh):
    u_refs = refs[:nseq * nh]
    tz_ref, bs_ref, cs_ref, lr_ref, li_ref, d_ref, o_ref = refs[nseq * nh:nseq * nh + 7]
    scr = refs[nseq * nh + 7:]
    x_scr, y_scr, st_r, st_i = scr[:4]
    ar_scrs, ai_scrs, sr_scrs, si_scrs = (scr[4 + q * nseq:4 + (q + 1) * nseq] for q in range(4))
    o_scrs = scr[4 + 4 * nseq:]
    gb = tz_ref.shape[0]
    r = S5_R
    npair = gb // 2
    gph = LANES // S5_GROUP
    p = S5_STATE

    @pl.when(pl.program_id(1) == 0)
    def _():
        st_r[...] = jnp.zeros_like(st_r)
        st_i[...] = jnp.zeros_like(st_i)

    for b in range(nseq):
        for j in range(S5_L):
            for h in range(nh):
                rows = u_refs[b * nh + h][pl.ds(j, r, stride=S5_L), :]
                x_scr[b * gb + h * gph:b * gb + (h + 1) * gph, j * S5_GROUP:(j + 1) * S5_GROUP, :] = (
                    _bf(rows.T).reshape(gph, S5_GROUP, r))

    for b in range(nseq):
        for k in range(npair):
            a = []
            for g in (2 * k, 2 * k + 1):
                xg = x_scr[b * gb + g]
                y_scr[b * gb + g] = jnp.dot(tz_ref[g], xg, preferred_element_type=jnp.float32)
                a.append(jnp.dot(bs_ref[g], xg, preferred_element_type=jnp.float32))
            ar_scrs[b][k * r:(k + 1) * r, :] = jnp.concatenate([a[0][:p], a[1][:p]], axis=0).T
            ai_scrs[b][k * r:(k + 1) * r, :] = jnp.concatenate([a[0][p:], a[1][p:]], axis=0).T

    lr, li = lr_ref[...], li_ref[...]
    srs = [st_r[b] for b in range(nseq)]
    sis = [st_i[b] for b in range(nseq)]
    for n in range(r):
        for b in range(nseq):
            sr, si = srs[b], sis[b]
            sr_scrs[b][pl.ds(n, npair, stride=r), :] = sr
            si_scrs[b][pl.ds(n, npair, stride=r), :] = si
            a_r = ar_scrs[b][pl.ds(n, npair, stride=r), :]
            a_i = ai_scrs[b][pl.ds(n, npair, stride=r), :]
            srs[b], sis[b] = sr * lr - si * li + a_r, sr * li + si * lr + a_i
    for b in range(nseq):
        st_r[b] = srs[b]
        st_i[b] = sis[b]

    for b in range(nseq):
        for k in range(npair):
            srt = sr_scrs[b][k * r:(k + 1) * r, :].T
            sit = si_scrs[b][k * r:(k + 1) * r, :].T
            s0 = _bf(jnp.concatenate([srt[:p], sit[:p]], axis=0))
            s1 = _bf(jnp.concatenate([srt[p:], sit[p:]], axis=0))
            y_scr[b * gb + 2 * k] += jnp.dot(cs_ref[2 * k], s0, preferred_element_type=jnp.float32)
            y_scr[b * gb + 2 * k + 1] += jnp.dot(cs_ref[2 * k + 1], s1, preferred_element_type=jnp.float32)

    for b in range(nseq):
        for j in range(S5_L):
            for h in range(nh):
                yj = y_scr[b * gb + h * gph:b * gb + (h + 1) * gph, j * S5_GROUP:(j + 1) * S5_GROUP, :]
                yj = yj.reshape(LANES, r).T
                uj = u_refs[b * nh + h][pl.ds(j, r, stride=S5_L), :]
                dsk = d_ref[:, h * LANES:(h + 1) * LANES]
                o_scrs[b * nh + h][pl.ds(j, r, stride=S5_L), :] = _gelu_tanh(yj + dsk * uj)
        for h in range(nh):
            o_ref[b, :, h * LANES:(h + 1) * LANES] = o_scrs[b * nh + h][...]


def _s5(proj, tz, bs, cs, l16r, l16i, d_skip, bsz, seq):
    g_total = tz.shape[0]
    width = g_total * S5_GROUP
    ch = S5_GB * S5_GROUP
    nh = ch // LANES
    tile = S5_L * S5_R
    nt = seq // tile
    p2 = 2 * S5_STATE
    lc = S5_L * S5_GROUP
    npair = S5_GB // 2
    kern = functools.partial(_s5_kernel, nseq=bsz, nh=nh)
    u_specs = [pl.BlockSpec((tile, LANES), lambda g, i, b=b, h=h: (b * nt + i, g * nh + h))
               for b in range(bsz) for h in range(nh)]
    state_rows = pltpu.VMEM((npair * S5_R, p2), jnp.float32)
    out = pl.pallas_call(
        kern,
        grid=(g_total // S5_GB, nt),
        in_specs=u_specs + [
            pl.BlockSpec((S5_GB, lc, lc), lambda g, i: (g, 0, 0)),
            pl.BlockSpec((S5_GB, p2, lc), lambda g, i: (g, 0, 0)),
            pl.BlockSpec((S5_GB, lc, p2), lambda g, i: (g, 0, 0)),
            pl.BlockSpec((npair, p2), lambda g, i: (g, 0)),
            pl.BlockSpec((npair, p2), lambda g, i: (g, 0)),
            pl.BlockSpec((1, ch), lambda g, i: (0, g))],
        out_specs=pl.BlockSpec((bsz, tile, ch), lambda g, i: (0, i, g)),
        out_shape=jax.ShapeDtypeStruct((bsz, seq, width), jnp.float32),
        scratch_shapes=[pltpu.VMEM((bsz * S5_GB, lc, S5_R), jnp.bfloat16),
                        pltpu.VMEM((bsz * S5_GB, lc, S5_R), jnp.float32),
                        pltpu.VMEM((bsz, npair, p2), jnp.float32),
                        pltpu.VMEM((bsz, npair, p2), jnp.float32)]
        + [state_rows for _ in range(4 * bsz)]
        + [pltpu.VMEM((tile, LANES), jnp.float32) for _ in range(bsz * nh)],
        compiler_params=pltpu.CompilerParams(
            dimension_semantics=("parallel", "arbitrary"), vmem_limit_bytes=VMEM_LIMIT),
        name="s5",
    )(*([proj] * (bsz * nh)), tz, bs, cs, l16r, l16i, d_skip.reshape(1, width))
    return out.reshape(bsz * seq, width)


def _s5_operators(lam_re, lam_im, log_step, b_re, b_im, c_re, c_im):
    f32 = jnp.float32
    hi = lax.Precision.HIGHEST
    g = lam_re.shape[0]
    lr = jnp.minimum(lam_re.astype(f32), LAMBDA_RE_MAX)
    li = lam_im.astype(f32)
    dt = jnp.exp(log_step.astype(f32))[:, None]
    nl, nc, npz = S5_L, S5_GROUP, S5_STATE
    lc = nl * nc
    a, w = lr * dt, li * dt

    def lam_pow(a_, w_, tau):
        mag = jnp.exp(a_ * tau)
        return mag * jnp.cos(w_ * tau), mag * jnp.sin(w_ * tau)

    p1r, p1i = lam_pow(a, w, 1.0)
    nr, ni = p1r - 1.0, p1i
    den = lr * lr + li * li
    kr, ki = (nr * lr + ni * li) / den, (ni * lr - nr * li) / den
    br, bi = b_re.astype(f32), b_im.astype(f32)
    bbr = kr[..., None] * br - ki[..., None] * bi
    bbi = kr[..., None] * bi + ki[..., None] * br
    cr, ci = c_re.astype(f32), c_im.astype(f32)
    per, pei = lam_pow(a[..., None], w[..., None], jnp.arange(nl - 1, -1, -1, dtype=f32))
    er = per[..., None] * bbr[:, :, None, :] - pei[..., None] * bbi[:, :, None, :]
    ei = per[..., None] * bbi[:, :, None, :] + pei[..., None] * bbr[:, :, None, :]
    bs = jnp.concatenate([er.reshape(g, npz, lc), ei.reshape(g, npz, lc)], axis=1)
    krev = jnp.einsum('gcq,gqx->gcx', jnp.concatenate([cr, -ci], axis=-1), bs, precision=hi)
    kpad = jnp.concatenate([krev, jnp.zeros((g, nc, lc - nc), f32)], axis=-1)
    tz = jnp.stack([kpad[:, :, (nl - 1 - j) * nc:(2 * nl - 1 - j) * nc] for j in range(nl)], axis=1)
    tz = tz.reshape(g, lc, lc)
    pcr, pci = lam_pow(a[:, None, :], w[:, None, :], jnp.arange(1, nl + 1, dtype=f32)[:, None])
    cpr = cr[:, None] * pcr[:, :, None, :] - ci[:, None] * pci[:, :, None, :]
    cpi = cr[:, None] * pci[:, :, None, :] + ci[:, None] * pcr[:, :, None, :]
    cs = jnp.concatenate([cpr, -cpi], axis=-1).reshape(g, lc, 2 * npz)
    plr, pli = lam_pow(a, w, float(nl))
    l16r = plr.reshape(g // 2, 2 * npz)
    l16i = pli.reshape(g // 2, 2 * npz)
    return _bf(tz), _bf(bs), _bf(cs), l16r, l16i


def _sigmoid_tanh(x):
    return 0.5 + 0.5 * jnp.tanh(0.5 * x)


def _glu_kernel(a_ref, w_ref, gt_ref, b_ref, o_ref, a_bf, z_scr):
    nct, tm, tn = a_bf.shape
    j = pl.program_id(1)

    @pl.when(j == 0)
    def _():
        for c in range(nct):
            a_bf[c] = _bf(a_ref[:, c * tn:(c + 1) * tn])

    hm = tm // 2
    for r0 in (0, hm):
        rows = slice(r0, r0 + hm)
        acc = jnp.dot(a_bf[0, rows, :], w_ref[:tn, :], preferred_element_type=jnp.float32)
        for c in range(1, nct):
            acc += jnp.dot(a_bf[c, rows, :], w_ref[c * tn:(c + 1) * tn, :], preferred_element_type=jnp.float32)
        z_scr[rows, :] = acc + b_ref[...]
        g = gt_ref[rows, :]
        gate = _sigmoid_tanh(z_scr[rows, :]) * (g * _sigmoid_tanh(g))
        o_ref[0, rows, :] = _bf(a_bf[j, rows, :].astype(jnp.float32) * gate)


def _glu(y, w, bias, proj, tm, tn):
    t, e = y.shape
    half = e // 2
    nh = half // tn
    goff = e // tn
    return pl.pallas_call(
        _glu_kernel,
        grid=(t // tm, e // tn),
        in_specs=[pl.BlockSpec((tm, e), lambda i, j: (i, 0)),
                  pl.BlockSpec((e, tn), lambda i, j: (0, j)),
                  pl.BlockSpec((tm, tn), lambda i, j: (i, goff + j)),
                  pl.BlockSpec((1, tn), lambda i, j: (0, j))],
        out_specs=pl.BlockSpec((1, tm, tn), lambda i, j: (j // nh, i, j % nh)),
        out_shape=jax.ShapeDtypeStruct((2, t, half), jnp.bfloat16),
        scratch_shapes=[pltpu.VMEM((e // tn, tm, tn), jnp.bfloat16), pltpu.VMEM((tm, tn), jnp.float32)],
        compiler_params=pltpu.CompilerParams(
            dimension_semantics=("parallel", "arbitrary"), vmem_limit_bytes=VMEM_LIMIT),
        name="glu",
    )(y, w, proj, bias.reshape(1, e))


def kernel(x, ev_w_in, ev_conv_w, ev_hg_norm, ev_w_out, ev_ln_g, ev_ln_b, hg_lb_logits, od_w_in, od_lam_re,
           od_lam_im, od_log_step, od_b_re, od_b_im, od_c_re, od_c_im, od_d, od_w_glu, od_b_glu, od_w_out,
           od_ln_g, od_ln_b):
    bsz, seq, d = x.shape
    t = bsz * seq
    f32 = jnp.float32
    h0 = x.reshape(t, d).astype(f32)

    lb_all = jnp.cumsum(jax.nn.softmax(hg_lb_logits.astype(f32), axis=0), axis=0)

    proj0 = _matmul(h0, _bf(ev_w_in[0]), f32, tm=512, tn=1024)
    y0 = _mixer0(proj0, ev_conv_w[0].astype(f32), lb_all[0], ev_hg_norm[0].astype(f32), bsz, seq)
    h1, h1_bf = _out_ln(y0, _bf(ev_w_out[0]), h0, ev_ln_g[0].astype(f32), ev_ln_b[0].astype(f32),
                        tm=256, tn=512, with_bf16_copy=True)

    proj1 = _matmul(h1_bf, _bf(od_w_in[0]), f32, tm=1024, tn=1024)
    tz, bs, cs, l16r, l16i = _s5_operators(od_lam_re[0], od_lam_im[0], od_log_step[0], od_b_re[0], od_b_im[0],
                                           od_c_re[0], od_c_im[0])
    ys = _s5(proj1, tz, bs, cs, l16r, l16i, od_d[0].astype(f32), bsz, seq)
    y1 = _glu(ys, _bf(od_w_glu[0]), od_b_glu[0].astype(f32), proj1, tm=512, tn=1024)
    h2, = _out_ln(y1, _bf(od_w_out[0]), h1, od_ln_g[0].astype(f32), od_ln_b[0].astype(f32),
                  tm=256, tn=512, with_bf16_copy=False)
    return h2.reshape(bsz, seq, d).astype(x.dtype)
```

```python
import functools
import math

import jax
import jax.numpy as jnp
from jax import lax
from jax.experimental import pallas as pl
from jax.experimental.pallas import tpu as pltpu

DEPTH = 2
ALPHA = (2 * DEPTH) ** 0.25
LN_EPS = 1e-5
RMS_EPS = 1e-6
LAMBDA_RE_MAX = -1e-4
CONV_K = 3
HG_HEADS = 16
HG_CHUNK = 32
S5_GROUP = 16
S5_STATE = 64

LANES = 128
SUBLANES = 8
VMEM_LIMIT = 61 * 1024 * 1024

MIX_TB = 256
MIX_HB = 8
S5_L = 16
S5_R = 128
S5_GB = 16

_NT = (((1,), (1,)), ((), ()))


def _bf(x):
    return x.astype(jnp.bfloat16)


def _mm_kernel(a_ref, b_ref, o_ref, *maybe_a_bf):
    if maybe_a_bf:
        a_bf, = maybe_a_bf

        @pl.when(pl.program_id(1) == 0)
        def _():
            a_bf[...] = _bf(a_ref[...])
    else:
        a_bf = a_ref
    o_ref[...] = jnp.dot(a_bf[...], b_ref[...], preferred_element_type=jnp.float32).astype(o_ref.dtype)


def _col_tiles(w, tn):
    k, n = w.shape
    return _bf(w).reshape(k, n // tn, tn).transpose(1, 0, 2)


def _matmul(a, b_tiles, out_dtype, tm):
    m, k = a.shape
    nt, _, tn = b_tiles.shape
    n = nt * tn
    scratch = [] if a.dtype == jnp.bfloat16 else [pltpu.VMEM((tm, k), jnp.bfloat16)]
    return pl.pallas_call(
        _mm_kernel,
        grid=(m // tm, nt),
        in_specs=[pl.BlockSpec((tm, k), lambda i, j: (i, 0)),
                  pl.BlockSpec((None, k, tn), lambda i, j: (j, 0, 0))],
        out_specs=pl.BlockSpec((tm, tn), lambda i, j: (i, j)),
        out_shape=jax.ShapeDtypeStruct((m, n), out_dtype),
        scratch_shapes=scratch,
        compiler_params=pltpu.CompilerParams(
            dimension_semantics=("parallel", "arbitrary"), vmem_limit_bytes=VMEM_LIMIT),
        name="in_proj",
    )(a, b_tiles)


LN_ROWS = 4 * SUBLANES


def _out_ln_kernel(y_ref, w_ref, r_ref, g_ref, b_ref, o_ref, *maybe_obf, tn):
    eh = y_ref.shape[2]
    tm, d = o_ref.shape
    for c in range(d // tn):
        cols = slice(c * tn, (c + 1) * tn)
        acc = jnp.dot(y_ref[0], w_ref[:eh, cols], preferred_element_type=jnp.float32)
        acc += jnp.dot(y_ref[1], w_ref[eh:, cols], preferred_element_type=jnp.float32)
        o_ref[:, cols] = ALPHA * r_ref[:, cols] + acc

    def ln_slab(r0):
        for rb in range(LN_ROWS // SUBLANES):
            rows = slice(r0 + rb * SUBLANES, r0 + (rb + 1) * SUBLANES)
            z = o_ref[rows, :]
            mu = jnp.mean(z, axis=-1, keepdims=True)
            zc = z - mu
            var = jnp.mean(zc * zc, axis=-1, keepdims=True)
            o_ref[rows, :] = zc * lax.rsqrt(var + LN_EPS) * g_ref[...] + b_ref[...]
        if maybe_obf:
            maybe_obf[0][r0:r0 + LN_ROWS, :] = _bf(o_ref[r0:r0 + LN_ROWS, :])

    for r0 in range(0, tm, LN_ROWS):
        pl.when(pl.program_id(0) >= 0)(functools.partial(ln_slab, r0))


def _out_ln(y2, w, resid, g, b, tm, tn, with_bf16_copy):
    _, t, eh = y2.shape
    e, d = w.shape
    row_spec = pl.BlockSpec((tm, d), lambda i: (i, 0))
    out_shape = [jax.ShapeDtypeStruct((t, d), jnp.float32)]
    if with_bf16_copy:
        out_shape.append(jax.ShapeDtypeStruct((t, d), jnp.bfloat16))
    return pl.pallas_call(
        functools.partial(_out_ln_kernel, tn=tn),
        grid=(t // tm,),
        in_specs=[pl.BlockSpec((2, tm, eh), lambda i: (0, i, 0)),
                  pl.BlockSpec((e, d), lambda i: (0, 0), pipeline_mode=pl.Buffered(1)),
                  row_spec,
                  pl.BlockSpec((1, d), lambda i: (0, 0)),
                  pl.BlockSpec((1, d), lambda i: (0, 0))],
        out_specs=[row_spec] * len(out_shape),
        out_shape=out_shape,
        compiler_params=pltpu.CompilerParams(
            dimension_semantics=("arbitrary",), vmem_limit_bytes=VMEM_LIMIT),
        name="out_proj_ln",
    )(y2, w, resid, g.reshape(1, d), b.reshape(1, d))


def _chunk_cumsum(x, row_in_chunk):
    sh = 1
    while sh < HG_CHUNK:
        x = x + jnp.where(row_in_chunk >= sh, pltpu.roll(x, sh, 0), 0.0)
        sh *= 2
    return x


def _mixer0_kernel(ab_ref, ac_ref, ah_ref, q_ref, f_ref, v_ref, ga_ref, gb_ref,
                   cw_ref, lb_ref, nw_ref, o_ref, carry, state):
    tb = pl.program_id(2)
    tbk, w = ab_ref.shape
    nc = tbk // HG_CHUNK

    @pl.when(tb == 0)
    def _():
        carry[...] = jnp.zeros_like(carry)
        state[...] = jnp.zeros_like(state)

    p = ac_ref[...] * ah_ref[...]
    row = lax.broadcasted_iota(jnp.int32, (tbk, w), 0)
    c6 = carry[SUBLANES - 2:SUBLANES - 1, :]
    c7 = carry[SUBLANES - 1:SUBLANES, :]
    p1 = jnp.where(row == 0, c7, pltpu.roll(p, 1, 0))
    p2 = jnp.where(row == 0, c6, jnp.where(row == 1, c7, pltpu.roll(p, 2, 0)))
    conv = cw_ref[0:1, :] * p2 + cw_ref[1:2, :] * p1 + cw_ref[2:3, :] * p
    carry[...] = p[tbk - SUBLANES:, :]
    o_ref[0] = _bf(ab_ref[...] * conv * jax.nn.silu(ga_ref[...]))

    rowh = lax.broadcasted_iota(jnp.int32, (tbk, LANES), 0)
    ric = rowh % HG_CHUNK
    r2 = lax.broadcasted_iota(jnp.int32, (tbk, tbk), 0)
    c2 = lax.broadcasted_iota(jnp.int32, (tbk, tbk), 1)
    causal = (r2 // HG_CHUNK == c2 // HG_CHUNK) & (c2 <= r2)
    lane_chunk = lax.broadcasted_iota(jnp.int32, (LANES, tbk), 1) // HG_CHUNK
    outs = []
    for h in range(w // LANES):
        sl = slice(h * LANES, (h + 1) * LANES)
        q = q_ref[:, sl]
        v = v_ref[:, sl]
        lb = lb_ref[:, sl]
        f = lb + (1.0 - lb) * jax.nn.sigmoid(f_ref[:, sl])
        kk = 1.0 - f
        b = _chunk_cumsum(jnp.log(f), ric)
        b_last_rows = [b[(c + 1) * HG_CHUNK - 1:(c + 1) * HG_CHUNK, :] for c in range(nc)]
        b_last = jnp.concatenate([jnp.broadcast_to(r, (HG_CHUNK, LANES)) for r in b_last_rows], axis=0)
        q_in = _bf(q * jnp.exp(b))
        k_in = _bf(kk * jnp.exp(-b))
        k_dec = _bf(kk * jnp.exp(b_last - b))
        vb = _bf(v)
        s = lax.dot_general(q_in, k_in, _NT, preferred_element_type=jnp.float32)
        s = _bf(jnp.where(causal, s, 0.0))
        o_intra = jnp.dot(s, vb, preferred_element_type=jnp.float32)
        v_t = _bf(v.T)
        zero_t = jnp.zeros_like(v_t)
        v_stack = jnp.concatenate([jnp.where(lane_chunk == c, v_t, zero_t) for c in range(nc)], axis=0)
        upd = jnp.dot(v_stack, k_dec, preferred_element_type=jnp.float32)
        st = state[h]
        sts = []
        for c in range(nc):
            sts.append(_bf(st))
            st = st * jnp.exp(b_last_rows[c]) + upd[c * LANES:(c + 1) * LANES]
        state[h] = st
        st_all = jnp.concatenate(sts, axis=1)
        zero_q = jnp.zeros_like(q_in)
        q_exp = jnp.concatenate([jnp.where(rowh // HG_CHUNK == c, q_in, zero_q) for c in range(nc)], axis=1)
        o_inter = lax.dot_general(q_exp, st_all, _NT, preferred_element_type=jnp.float32)
        o = o_intra + o_inter
        o = o * lax.rsqrt(jnp.mean(o * o, axis=-1, keepdims=True) + RMS_EPS) * nw_ref[...]
        outs.append(o)
    ob = outs[0] if len(outs) == 1 else jnp.concatenate(outs, axis=1)
    o_ref[1] = _bf(ob * jax.nn.silu(gb_ref[...]))


def _mixer0(proj, conv_w, lb, hg_norm, bsz, seq):
    t, e_in = proj.shape
    cw = conv_w.shape[1]
    w = MIX_HB * LANES
    nsec = cw // w
    nt = seq // MIX_TB

    def sec(k):
        return pl.BlockSpec((MIX_TB, w), lambda b, g, s, k=k: (b * nt + s, k * nsec + g))

    return pl.pallas_call(
        _mixer0_kernel,
        grid=(bsz, nsec, nt),
        in_specs=[sec(0), sec(1), sec(2), sec(3), sec(4), sec(5), sec(6), sec(7),
                  pl.BlockSpec((CONV_K, w), lambda b, g, s: (0, g)),
                  pl.BlockSpec((1, w), lambda b, g, s: (0, g)),
                  pl.BlockSpec((1, LANES), lambda b, g, s: (0, 0))],
        out_specs=pl.BlockSpec((2, MIX_TB, w), lambda b, g, s: (0, b * nt + s, g)),
        out_shape=jax.ShapeDtypeStruct((2, t, cw), jnp.bfloat16),
        scratch_shapes=[pltpu.VMEM((SUBLANES, w), jnp.float32),
                        pltpu.VMEM((MIX_HB, LANES, LANES), jnp.float32)],
        compiler_params=pltpu.CompilerParams(
            dimension_semantics=("parallel", "parallel", "arbitrary"), vmem_limit_bytes=VMEM_LIMIT),
        name="mixer0",
    )(proj, proj, proj, proj, proj, proj, proj, proj, conv_w, lb.reshape(1, cw), hg_norm.reshape(1, LANES))


def _gelu_tanh(x):
    return 0.5 * x * (1.0 + jnp.tanh(math.sqrt(2.0 / math.pi) * (x + 0.044715 * (x * x * x))))


def _s5_kernel(*refs, nseq, nh):
    u_refs = refs[:nseq * nh]
    tz_ref, bs_ref, cs_ref, lr_ref, li_ref, d_ref, o_ref = refs[nseq * nh:nseq * nh + 7]
    scr = refs[nseq * nh + 7:]
    x_scr, y_scr, st_r, st_i = scr[:4]
    ar_scrs, ai_scrs, sr_scrs, si_scrs = (scr[4 + q * nseq:4 + (q + 1) * nseq] for q in range(4))
    o_scrs = scr[4 + 4 * nseq:]
    gb = tz_ref.shape[0]
    r = S5_R
    npair = gb // 2
    gph = LANES // S5_GROUP
    p = S5_STATE

    @pl.when(pl.program_id(1) == 0)
    def _():
        st_r[...] = jnp.zeros_like(st_r)
        st_i[...] = jnp.zeros_like(st_i)

    for b in range(nseq):
        for j in range(S5_L):
            for h in range(nh):
                rows = u_refs[b * nh + h][pl.ds(j, r, stride=S5_L), :]
                x_scr[b * gb + h * gph:b * gb + (h + 1) * gph, j * S5_GROUP:(j + 1) * S5_GROUP, :] = (
                    _bf(rows.T).reshape(gph, S5_GROUP, r))

    for b in range(nseq):
        for k in range(npair):
            a = []
            for g in (2 * k, 2 * k + 1):
                xg = x_scr[b * gb + g]
                y_scr[b * gb + g] = jnp.dot(tz_ref[g], xg, preferred_element_type=jnp.float32)
                a.append(jnp.dot(bs_ref[g], xg, preferred_element_type=jnp.float32))
            ar_scrs[b][k * r:(k + 1) * r, :] = jnp.concatenate([a[0][:p], a[1][:p]], axis=0).T
            ai_scrs[b][k * r:(k + 1) * r, :] = jnp.concatenate([a[0][p:], a[1][p:]], axis=0).T

    lr, li = lr_ref[...], li_ref[...]
    srs = [st_r[b] for b in range(nseq)]
    sis = [st_i[b] for b in range(nseq)]
    for n in range(r):
        for b in range(nseq):
            sr, si = srs[b], sis[b]
            sr_scrs[b][pl.ds(n, npair, stride=r), :] = sr
            si_scrs[b][pl.ds(n, npair, stride=r), :] = si
            a_r = ar_scrs[b][pl.ds(n, npair, stride=r), :]
            a_i = ai_scrs[b][pl.ds(n, npair, stride=r), :]
            srs[b], sis[b] = sr * lr - si * li + a_r, sr * li + si * lr + a_i
    for b in range(nseq):
        st_r[b] = srs[b]
        st_i[b] = sis[b]

    for b in range(nseq):
        for k in range(npair):
            srt = sr_scrs[b][k * r:(k + 1) * r, :].T
            sit = si_scrs[b][k * r:(k + 1) * r, :].T
            s0 = _bf(jnp.concatenate([srt[:p], sit[:p]], axis=0))
            s1 = _bf(jnp.concatenate([srt[p:], sit[p:]], axis=0))
            y_scr[b * gb + 2 * k] += jnp.dot(cs_ref[2 * k], s0, preferred_element_type=jnp.float32)
            y_scr[b * gb + 2 * k + 1] += jnp.dot(cs_ref[2 * k + 1], s1, preferred_element_type=jnp.float32)

    for b in range(nseq):
        for j in range(S5_L):
            for h in range(nh):
                yj = y_scr[b * gb + h * gph:b * gb + (h + 1) * gph, j * S5_GROUP:(j + 1) * S5_GROUP, :]
                yj = yj.reshape(LANES, r).T
                uj = u_refs[b * nh + h][pl.ds(j, r, stride=S5_L), :]
                dsk = d_ref[:, h * LANES:(h + 1) * LANES]
                o_scrs[b * nh + h][pl.ds(j, r, stride=S5_L), :] = _gelu_tanh(yj + dsk * uj)
        for h in range(nh):
            o_ref[b, :, h * LANES:(h + 1) * LANES] = o_scrs[b * nh + h][...]


def _s5(proj, tz, bs, cs, l16r, l16i, d_skip, bsz, seq):
    g_total = tz.shape[0]
    width = g_total * S5_GROUP
    ch = S5_GB * S5_GROUP
    nh = ch // LANES
    tile = S5_L * S5_R
    nt = seq // tile
    p2 = 2 * S5_STATE
    lc = S5_L * S5_GROUP
    npair = S5_GB // 2
    kern = functools.partial(_s5_kernel, nseq=bsz, nh=nh)
    u_specs = [pl.BlockSpec((tile, LANES), lambda g, i, b=b, h=h: (b * nt + i, g * nh + h))
               for b in range(bsz) for h in range(nh)]
    state_rows = pltpu.VMEM((npair * S5_R, p2), jnp.float32)
    out = pl.pallas_call(
        kern,
        grid=(g_total // S5_GB, nt),
        in_specs=u_specs + [
            pl.BlockSpec((S5_GB, lc, lc), lambda g, i: (g, 0, 0)),
            pl.BlockSpec((S5_GB, p2, lc), lambda g, i: (g, 0, 0)),
            pl.BlockSpec((S5_GB, lc, p2), lambda g, i: (g, 0, 0)),
            pl.BlockSpec((npair, p2), lambda g, i: (g, 0)),
            pl.BlockSpec((npair, p2), lambda g, i: (g, 0)),
            pl.BlockSpec((1, ch), lambda g, i: (0, g))],
        out_specs=pl.BlockSpec((bsz, tile, ch), lambda g, i: (0, i, g)),
        out_shape=jax.ShapeDtypeStruct((bsz, seq, width), jnp.float32),
        scratch_shapes=[pltpu.VMEM((bsz * S5_GB, lc, S5_R), jnp.bfloat16),
                        pltpu.VMEM((bsz * S5_GB, lc, S5_R), jnp.float32),
                        pltpu.VMEM((bsz, npair, p2), jnp.float32),
                        pltpu.VMEM((bsz, npair, p2), jnp.float32)]
        + [state_rows for _ in range(4 * bsz)]
        + [pltpu.VMEM((tile, LANES), jnp.float32) for _ in range(bsz * nh)],
        compiler_params=pltpu.CompilerParams(
            dimension_semantics=("parallel", "arbitrary"), vmem_limit_bytes=VMEM_LIMIT),
        name="s5",
    )(*([proj] * (bsz * nh)), tz, bs, cs, l16r, l16i, d_skip.reshape(1, width))
    return out.reshape(bsz * seq, width)


def _s5_operators(lam_re, lam_im, log_step, b_re, b_im, c_re, c_im):
    f32 = jnp.float32
    hi = lax.Precision.HIGHEST
    g = lam_re.shape[0]
    lr = jnp.minimum(lam_re.astype(f32), LAMBDA_RE_MAX)
    li = lam_im.astype(f32)
    dt = jnp.exp(log_step.astype(f32))[:, None]
    nl, nc, npz = S5_L, S5_GROUP, S5_STATE
    lc = nl * nc
    a, w = lr * dt, li * dt

    def lam_pow(a_, w_, tau):
        mag = jnp.exp(a_ * tau)
        return mag * jnp.cos(w_ * tau), mag * jnp.sin(w_ * tau)

    p1r, p1i = lam_pow(a, w, 1.0)
    nr, ni = p1r - 1.0, p1i
    den = lr * lr + li * li
    kr, ki = (nr * lr + ni * li) / den, (ni * lr - nr * li) / den
    br, bi = b_re.astype(f32), b_im.astype(f32)
    bbr = kr[..., None] * br - ki[..., None] * bi
    bbi = kr[..., None] * bi + ki[..., None] * br
    cr, ci = c_re.astype(f32), c_im.astype(f32)
    per, pei = lam_pow(a[..., None], w[..., None], jnp.arange(nl - 1, -1, -1, dtype=f32))
    er = per[..., None] * bbr[:, :, None, :] - pei[..., None] * bbi[:, :, None, :]
    ei = per[..., None] * bbi[:, :, None, :] + pei[..., None] * bbr[:, :, None, :]
    bs = jnp.concatenate([er.reshape(g, npz, lc), ei.reshape(g, npz, lc)], axis=1)
    krev = jnp.einsum('gcq,gqx->gcx', jnp.concatenate([cr, -ci], axis=-1), bs, precision=hi)
    kpad = jnp.concatenate([krev, jnp.zeros((g, nc, lc - nc), f32)], axis=-1)
    tz = jnp.stack([kpad[:, :, (nl - 1 - j) * nc:(2 * nl - 1 - j) * nc] for j in range(nl)], axis=1)
    tz = tz.reshape(g, lc, lc)
    pcr, pci = lam_pow(a[:, None, :], w[:, None, :], jnp.arange(1, nl + 1, dtype=f32)[:, None])
    cpr = cr[:, None] * pcr[:, :, None, :] - ci[:, None] * pci[:, :, None, :]
    cpi = cr[:, None] * pci[:, :, None, :] + ci[:, None] * pcr[:, :, None, :]
    cs = jnp.concatenate([cpr, -cpi], axis=-1).reshape(g, lc, 2 * npz)
    plr, pli = lam_pow(a, w, float(nl))
    l16r = plr.reshape(g // 2, 2 * npz)
    l16i = pli.reshape(g // 2, 2 * npz)
    return _bf(tz), _bf(bs), _bf(cs), l16r, l16i


def _sigmoid_tanh(x):
    return 0.5 + 0.5 * jnp.tanh(0.5 * x)


def _glu_kernel(a_ref, w_ref, gt_ref, b_ref, o_ref, a_bf, z_scr):
    nct, tm, tn = a_bf.shape
    j = pl.program_id(1)

    @pl.when(j == 0)
    def _():
        for c in range(nct):
            a_bf[c] = _bf(a_ref[:, c * tn:(c + 1) * tn])

    hm = tm // 2
    for r0 in (0, hm):
        rows = slice(r0, r0 + hm)
        acc = jnp.dot(a_bf[0, rows, :], w_ref[:tn, :], preferred_element_type=jnp.float32)
        for c in range(1, nct):
            acc += jnp.dot(a_bf[c, rows, :], w_ref[c * tn:(c + 1) * tn, :], preferred_element_type=jnp.float32)
        z_scr[rows, :] = acc + b_ref[...]
        g = gt_ref[rows, :]
        gate = _sigmoid_tanh(z_scr[rows, :]) * (g * _sigmoid_tanh(g))
        o_ref[0, rows, :] = _bf(a_bf[j, rows, :].astype(jnp.float32) * gate)


def _glu(y, w_tiles, bias, proj, tm):
    t, e = y.shape
    tn = w_tiles.shape[2]
    half = e // 2
    nh = half // tn
    goff = e // tn
    return pl.pallas_call(
        _glu_kernel,
        grid=(t // tm, e // tn),
        in_specs=[pl.BlockSpec((tm, e), lambda i, j: (i, 0)),
                  pl.BlockSpec((None, e, tn), lambda i, j: (j, 0, 0)),
                  pl.BlockSpec((tm, tn), lambda i, j: (i, goff + j)),
                  pl.BlockSpec((1, tn), lambda i, j: (0, j))],
        out_specs=pl.BlockSpec((1, tm, tn), lambda i, j: (j // nh, i, j % nh)),
        out_shape=jax.ShapeDtypeStruct((2, t, half), jnp.bfloat16),
        scratch_shapes=[pltpu.VMEM((e // tn, tm, tn), jnp.bfloat16), pltpu.VMEM((tm, tn), jnp.float32)],
        compiler_params=pltpu.CompilerParams(
            dimension_semantics=("parallel", "arbitrary"), vmem_limit_bytes=VMEM_LIMIT),
        name="glu",
    )(y, w_tiles, proj, bias.reshape(1, e))


def kernel(x, ev_w_in, ev_conv_w, ev_hg_norm, ev_w_out, ev_ln_g, ev_ln_b, hg_lb_logits, od_w_in, od_lam_re,
           od_lam_im, od_log_step, od_b_re, od_b_im, od_c_re, od_c_im, od_d, od_w_glu, od_b_glu, od_w_out,
           od_ln_g, od_ln_b):
    bsz, seq, d = x.shape
    t = bsz * seq
    f32 = jnp.float32
    h0 = x.reshape(t, d).astype(f32)

    lb_all = jnp.cumsum(jax.nn.softmax(hg_lb_logits.astype(f32), axis=0), axis=0)

    proj0 = _matmul(h0, _col_tiles(ev_w_in[0], 1024), f32, tm=512)
    y0 = _mixer0(proj0, ev_conv_w[0].astype(f32), lb_all[0], ev_hg_norm[0].astype(f32), bsz, seq)
    h1, h1_bf = _out_ln(y0, _bf(ev_w_out[0]), h0, ev_ln_g[0].astype(f32), ev_ln_b[0].astype(f32),
                        tm=256, tn=512, with_bf16_copy=True)

    proj1 = _matmul(h1_bf, _col_tiles(od_w_in[0], 1024), f32, tm=1024)
    tz, bs, cs, l16r, l16i = _s5_operators(od_lam_re[0], od_lam_im[0], od_log_step[0], od_b_re[0], od_b_im[0],
                                           od_c_re[0], od_c_im[0])
    ys = _s5(proj1, tz, bs, cs, l16r, l16i, od_d[0].astype(f32), bsz, seq)
    y1 = _glu(ys, _col_tiles(od_w_glu[0], 1024), od_b_glu[0].astype(f32), proj1, tm=512)
    h2, = _out_ln(y1, _bf(od_w_out[0]), h1, od_ln_g[0].astype(f32), od_ln_b[0].astype(f32),
                  tm=256, tn=512, with_bf16_copy=False)
    return h2.reshape(bsz, seq, d).astype(x.dtype)
```

```python
import functools
import math

import jax
import jax.numpy as jnp
from jax import lax
from jax.experimental import pallas as pl
from jax.experimental.pallas import tpu as pltpu

DEPTH = 2
ALPHA = (2 * DEPTH) ** 0.25
LN_EPS = 1e-5
RMS_EPS = 1e-6
LAMBDA_RE_MAX = -1e-4
CONV_K = 3
HG_HEADS = 16
HG_CHUNK = 32
S5_GROUP = 16
S5_STATE = 64

LANES = 128
SUBLANES = 8
VMEM_LIMIT = 61 * 1024 * 1024

MIX_TB = 256
MIX_HB = 8
S5_L = 16
S5_R = 128
S5_GB = 16
S5_RP = S5_R + SUBLANES
GLU_TN = 1024

_NT = (((1,), (1,)), ((), ()))


def _bf(x):
    return x.astype(jnp.bfloat16)


def _mm_kernel(a_ref, b_ref, o_ref, *maybe_a_bf):
    if maybe_a_bf:
        a_bf, = maybe_a_bf

        @pl.when(pl.program_id(1) == 0)
        def _():
            a_bf[...] = _bf(a_ref[...])
    else:
        a_bf = a_ref
    o_ref[...] = jnp.dot(a_bf[...], b_ref[...], preferred_element_type=jnp.float32).astype(o_ref.dtype)


def _matmul(a, b, out_dtype, tm, tn):
    m, k = a.shape
    n = b.shape[1]
    scratch = [] if a.dtype == jnp.bfloat16 else [pltpu.VMEM((tm, k), jnp.bfloat16)]
    return pl.pallas_call(
        _mm_kernel,
        grid=(m // tm, n // tn),
        in_specs=[pl.BlockSpec((tm, k), lambda i, j: (i, 0)),
                  pl.BlockSpec((k, tn), lambda i, j: (0, j))],
        out_specs=pl.BlockSpec((tm, tn), lambda i, j: (i, j)),
        out_shape=jax.ShapeDtypeStruct((m, n), out_dtype),
        scratch_shapes=scratch,
        compiler_params=pltpu.CompilerParams(
            dimension_semantics=("parallel", "arbitrary"), vmem_limit_bytes=VMEM_LIMIT),
        name="in_proj",
    )(a, b)


LN_ROWS = 4 * SUBLANES


def _out_ln_kernel(y_ref, w_ref, r_ref, g_ref, b_ref, o_ref, *maybe_obf, tn):
    eh = y_ref.shape[2]
    tm, d = o_ref.shape
    for c in range(d // tn):
        cols = slice(c * tn, (c + 1) * tn)
        acc = jnp.dot(y_ref[0], w_ref[:eh, cols], preferred_element_type=jnp.float32)
        acc += jnp.dot(y_ref[1], w_ref[eh:, cols], preferred_element_type=jnp.float32)
        o_ref[:, cols] = ALPHA * r_ref[:, cols] + acc

    def ln_slab(r0):
        for rb in range(LN_ROWS // SUBLANES):
            rows = slice(r0 + rb * SUBLANES, r0 + (rb + 1) * SUBLANES)
            z = o_ref[rows, :]
            mu = jnp.mean(z, axis=-1, keepdims=True)
            zc = z - mu
            var = jnp.mean(zc * zc, axis=-1, keepdims=True)
            o_ref[rows, :] = zc * lax.rsqrt(var + LN_EPS) * g_ref[...] + b_ref[...]
        if maybe_obf:
            maybe_obf[0][r0:r0 + LN_ROWS, :] = _bf(o_ref[r0:r0 + LN_ROWS, :])

    for r0 in range(0, tm, LN_ROWS):
        pl.when(pl.program_id(0) >= 0)(functools.partial(ln_slab, r0))


def _out_ln(y2, w, resid, g, b, tm, tn, with_bf16_copy):
    _, t, eh = y2.shape
    e, d = w.shape
    row_spec = pl.BlockSpec((tm, d), lambda i: (i, 0))
    out_shape = [jax.ShapeDtypeStruct((t, d), jnp.float32)]
    if with_bf16_copy:
        out_shape.append(jax.ShapeDtypeStruct((t, d), jnp.bfloat16))
    return pl.pallas_call(
        functools.partial(_out_ln_kernel, tn=tn),
        grid=(t // tm,),
        in_specs=[pl.BlockSpec((2, tm, eh), lambda i: (0, i, 0)),
                  pl.BlockSpec((e, d), lambda i: (0, 0), pipeline_mode=pl.Buffered(1)),
                  row_spec,
                  pl.BlockSpec((1, d), lambda i: (0, 0)),
                  pl.BlockSpec((1, d), lambda i: (0, 0))],
        out_specs=[row_spec] * len(out_shape),
        out_shape=out_shape,
        compiler_params=pltpu.CompilerParams(
            dimension_semantics=("arbitrary",), vmem_limit_bytes=VMEM_LIMIT),
        name="out_proj_ln",
    )(y2, w, resid, g.reshape(1, d), b.reshape(1, d))


def _chunk_cumsum(x, row_in_chunk):
    sh = 1
    while sh < HG_CHUNK:
        x = x + jnp.where(row_in_chunk >= sh, pltpu.roll(x, sh, 0), 0.0)
        sh *= 2
    return x


def _mixer0_kernel(ab_ref, ac_ref, ah_ref, q_ref, f_ref, v_ref, ga_ref, gb_ref,
                   cw_ref, lb_ref, nw_ref, o_ref, carry, state):
    tb = pl.program_id(2)
    tbk, w = ab_ref.shape
    nc = tbk // HG_CHUNK

    @pl.when(tb == 0)
    def _():
        carry[...] = jnp.zeros_like(carry)
        state[...] = jnp.zeros_like(state)

    p = ac_ref[...] * ah_ref[...]
    row = lax.broadcasted_iota(jnp.int32, (tbk, w), 0)
    c6 = carry[SUBLANES - 2:SUBLANES - 1, :]
    c7 = carry[SUBLANES - 1:SUBLANES, :]
    p1 = jnp.where(row == 0, c7, pltpu.roll(p, 1, 0))
    p2 = jnp.where(row == 0, c6, jnp.where(row == 1, c7, pltpu.roll(p, 2, 0)))
    conv = cw_ref[0:1, :] * p2 + cw_ref[1:2, :] * p1 + cw_ref[2:3, :] * p
    carry[...] = p[tbk - SUBLANES:, :]
    o_ref[0] = _bf(ab_ref[...] * conv * jax.nn.silu(ga_ref[...]))

    rowh = lax.broadcasted_iota(jnp.int32, (tbk, LANES), 0)
    ric = rowh % HG_CHUNK
    r2 = lax.broadcasted_iota(jnp.int32, (tbk, tbk), 0)
    c2 = lax.broadcasted_iota(jnp.int32, (tbk, tbk), 1)
    causal = (r2 // HG_CHUNK == c2 // HG_CHUNK) & (c2 <= r2)
    lane_chunk = lax.broadcasted_iota(jnp.int32, (LANES, tbk), 1) // HG_CHUNK
    outs = []
    for h in range(w // LANES):
        sl = slice(h * LANES, (h + 1) * LANES)
        q = q_ref[:, sl]
        v = v_ref[:, sl]
        lb = lb_ref[:, sl]
        f = lb + (1.0 - lb) * jax.nn.sigmoid(f_ref[:, sl])
        kk = 1.0 - f
        b = _chunk_cumsum(jnp.log(f), ric)
        b_last_rows = [b[(c + 1) * HG_CHUNK - 1:(c + 1) * HG_CHUNK, :] for c in range(nc)]
        b_last = jnp.concatenate([jnp.broadcast_to(r, (HG_CHUNK, LANES)) for r in b_last_rows], axis=0)
        q_in = _bf(q * jnp.exp(b))
        k_in = _bf(kk * jnp.exp(-b))
        k_dec = _bf(kk * jnp.exp(b_last - b))
        vb = _bf(v)
        s = lax.dot_general(q_in, k_in, _NT, preferred_element_type=jnp.float32)
        s = _bf(jnp.where(causal, s, 0.0))
        o_intra = jnp.dot(s, vb, preferred_element_type=jnp.float32)
        v_t = _bf(v.T)
        zero_t = jnp.zeros_like(v_t)
        v_stack = jnp.concatenate([jnp.where(lane_chunk == c, v_t, zero_t) for c in range(nc)], axis=0)
        upd = jnp.dot(v_stack, k_dec, preferred_element_type=jnp.float32)
        st = state[h]
        sts = []
        for c in range(nc):
            sts.append(_bf(st))
            st = st * jnp.exp(b_last_rows[c]) + upd[c * LANES:(c + 1) * LANES]
        state[h] = st
        st_all = jnp.concatenate(sts, axis=1)
        zero_q = jnp.zeros_like(q_in)
        q_exp = jnp.concatenate([jnp.where(rowh // HG_CHUNK == c, q_in, zero_q) for c in range(nc)], axis=1)
        o_inter = lax.dot_general(q_exp, st_all, _NT, preferred_element_type=jnp.float32)
        o = o_intra + o_inter
        o = o * lax.rsqrt(jnp.mean(o * o, axis=-1, keepdims=True) + RMS_EPS) * nw_ref[...]
        outs.append(o)
    ob = outs[0] if len(outs) == 1 else jnp.concatenate(outs, axis=1)
    o_ref[1] = _bf(ob * jax.nn.silu(gb_ref[...]))


def _mixer0(proj, conv_w, lb, hg_norm, bsz, seq):
    t, e_in = proj.shape
    cw = conv_w.shape[1]
    w = MIX_HB * LANES
    nsec = cw // w
    nt = seq // MIX_TB

    def sec(k):
        return pl.BlockSpec((MIX_TB, w), lambda b, g, s, k=k: (b * nt + s, k * nsec + g))

    return pl.pallas_call(
        _mixer0_kernel,
        grid=(bsz, nsec, nt),
        in_specs=[sec(0), sec(1), sec(2), sec(3), sec(4), sec(5), sec(6), sec(7),
                  pl.BlockSpec((CONV_K, w), lambda b, g, s: (0, g)),
                  pl.BlockSpec((1, w), lambda b, g, s: (0, g)),
                  pl.BlockSpec((1, LANES), lambda b, g, s: (0, 0))],
        out_specs=pl.BlockSpec((2, MIX_TB, w), lambda b, g, s: (0, b * nt + s, g)),
        out_shape=jax.ShapeDtypeStruct((2, t, cw), jnp.bfloat16),
        scratch_shapes=[pltpu.VMEM((SUBLANES, w), jnp.float32),
                        pltpu.VMEM((MIX_HB, LANES, LANES), jnp.float32)],
        compiler_params=pltpu.CompilerParams(
            dimension_semantics=("parallel", "parallel", "arbitrary"), vmem_limit_bytes=VMEM_LIMIT),
        name="mixer0",
    )(proj, proj, proj, proj, proj, proj, proj, proj, conv_w, lb.reshape(1, cw), hg_norm.reshape(1, LANES))


def _gelu_tanh(x):
    return 0.5 * x * (1.0 + jnp.tanh(math.sqrt(2.0 / math.pi) * (x + 0.044715 * (x * x * x))))


def _s5_kernel(*refs, nseq, nh):
    u_refs = refs[:nseq * nh]
    tz_ref, bs_ref, cs_ref, lr_ref, li_ref, d_ref, o_ref = refs[nseq * nh:nseq * nh + 7]
    scr = refs[nseq * nh + 7:]
    x_scr, y_scr, st_r, st_i = scr[:4]
    ar_scrs, ai_scrs, sr_scrs, si_scrs = (scr[4 + q * nseq:4 + (q + 1) * nseq] for q in range(4))
    o_scrs = scr[4 + 4 * nseq:]
    gb = tz_ref.shape[0]
    r = S5_R
    rp = S5_RP
    npair = gb // 2
    gph = LANES // S5_GROUP
    p = S5_STATE

    @pl.when(pl.program_id(1) == 0)
    def _():
        st_r[...] = jnp.zeros_like(st_r)
        st_i[...] = jnp.zeros_like(st_i)

    for b in range(nseq):
        for j in range(S5_L):
            for h in range(nh):
                rows = u_refs[b * nh + h][pl.ds(j, r, stride=S5_L), :]
                x_scr[b * gb + h * gph:b * gb + (h + 1) * gph, j * S5_GROUP:(j + 1) * S5_GROUP, :] = (
                    _bf(rows.T).reshape(gph, S5_GROUP, r))

    for b in range(nseq):
        for k in range(npair):
            a = []
            for g in (2 * k, 2 * k + 1):
                xg = x_scr[b * gb + g]
                y_scr[b * gb + g] = jnp.dot(tz_ref[g], xg, preferred_element_type=jnp.float32)
                a.append(jnp.dot(bs_ref[g], xg, preferred_element_type=jnp.float32))
            ar_scrs[b][k * rp:k * rp + r, :] = jnp.concatenate([a[0][:p], a[1][:p]], axis=0).T
            ai_scrs[b][k * rp:k * rp + r, :] = jnp.concatenate([a[0][p:], a[1][p:]], axis=0).T

    lr, li = lr_ref[...], li_ref[...]
    srs = [st_r[b] for b in range(nseq)]
    sis = [st_i[b] for b in range(nseq)]
    for n in range(r):
        for b in range(nseq):
            sr, si = srs[b], sis[b]
            sr_scrs[b][pl.ds(n, npair, stride=rp), :] = sr
            si_scrs[b][pl.ds(n, npair, stride=rp), :] = si
            a_r = ar_scrs[b][pl.ds(n, npair, stride=rp), :]
            a_i = ai_scrs[b][pl.ds(n, npair, stride=rp), :]
            srs[b], sis[b] = sr * lr - si * li + a_r, sr * li + si * lr + a_i
    for b in range(nseq):
        st_r[b] = srs[b]
        st_i[b] = sis[b]

    for b in range(nseq):
        for k in range(npair):
            srt = sr_scrs[b][k * rp:k * rp + r, :].T
            sit = si_scrs[b][k * rp:k * rp + r, :].T
            s0 = _bf(jnp.concatenate([srt[:p], sit[:p]], axis=0))
            s1 = _bf(jnp.concatenate([srt[p:], sit[p:]], axis=0))
            y_scr[b * gb + 2 * k] += jnp.dot(cs_ref[2 * k], s0, preferred_element_type=jnp.float32)
            y_scr[b * gb + 2 * k + 1] += jnp.dot(cs_ref[2 * k + 1], s1, preferred_element_type=jnp.float32)

    for b in range(nseq):
        for j in range(S5_L):
            for h in range(nh):
                yj = y_scr[b * gb + h * gph:b * gb + (h + 1) * gph, j * S5_GROUP:(j + 1) * S5_GROUP, :]
                yj = yj.reshape(LANES, r).T
                uj = u_refs[b * nh + h][pl.ds(j, r, stride=S5_L), :]
                dsk = d_ref[:, h * LANES:(h + 1) * LANES]
                o_scrs[b * nh + h][pl.ds(j, r, stride=S5_L), :] = _gelu_tanh(yj + dsk * uj)
        for h in range(nh):
            o_ref[b, :, h * LANES:(h + 1) * LANES] = _bf(o_scrs[b * nh + h][...])


def _s5(proj, tz, bs, cs, l16r, l16i, d_skip, bsz, seq):
    g_total = tz.shape[0]
    width = g_total * S5_GROUP
    ch = S5_GB * S5_GROUP
    nh = ch // LANES
    tile = S5_L * S5_R
    nt = seq // tile
    p2 = 2 * S5_STATE
    lc = S5_L * S5_GROUP
    npair = S5_GB // 2
    kern = functools.partial(_s5_kernel, nseq=bsz, nh=nh)
    u_specs = [pl.BlockSpec((tile, LANES), lambda g, i, b=b, h=h: (b * nt + i, g * nh + h))
               for b in range(bsz) for h in range(nh)]
    state_rows = pltpu.VMEM((npair * S5_RP, p2), jnp.float32)
    gpt = GLU_TN // ch
    out = pl.pallas_call(
        kern,
        grid=(g_total // S5_GB, nt),
        in_specs=u_specs + [
            pl.BlockSpec((S5_GB, lc, lc), lambda g, i: (g, 0, 0)),
            pl.BlockSpec((S5_GB, p2, lc), lambda g, i: (g, 0, 0)),
            pl.BlockSpec((S5_GB, lc, p2), lambda g, i: (g, 0, 0)),
            pl.BlockSpec((npair, p2), lambda g, i: (g, 0)),
            pl.BlockSpec((npair, p2), lambda g, i: (g, 0)),
            pl.BlockSpec((1, ch), lambda g, i: (0, g))],
        out_specs=pl.BlockSpec((None, bsz, tile, ch), lambda g, i: (g // gpt, 0, i, g % gpt)),
        out_shape=jax.ShapeDtypeStruct((width // GLU_TN, bsz, seq, GLU_TN), jnp.bfloat16),
        scratch_shapes=[pltpu.VMEM((bsz * S5_GB, lc, S5_R), jnp.bfloat16),
                        pltpu.VMEM((bsz * S5_GB, lc, S5_R), jnp.float32),
                        pltpu.VMEM((bsz, npair, p2), jnp.float32),
                        pltpu.VMEM((bsz, npair, p2), jnp.float32)]
        + [state_rows for _ in range(4 * bsz)]
        + [pltpu.VMEM((tile, LANES), jnp.float32) for _ in range(bsz * nh)],
        compiler_params=pltpu.CompilerParams(
            dimension_semantics=("parallel", "arbitrary"), vmem_limit_bytes=VMEM_LIMIT),
        name="s5",
    )(*([proj] * (bsz * nh)), tz, bs, cs, l16r, l16i, d_skip.reshape(1, width))
    return out.reshape(width // GLU_TN, bsz * seq, GLU_TN)


def _s5_operators(lam_re, lam_im, log_step, b_re, b_im, c_re, c_im):
    f32 = jnp.float32
    hi = lax.Precision.HIGHEST
    g = lam_re.shape[0]
    lr = jnp.minimum(lam_re.astype(f32), LAMBDA_RE_MAX)
    li = lam_im.astype(f32)
    dt = jnp.exp(log_step.astype(f32))[:, None]
    nl, nc, npz = S5_L, S5_GROUP, S5_STATE
    lc = nl * nc
    a, w = lr * dt, li * dt

    def lam_pow(a_, w_, tau):
        mag = jnp.exp(a_ * tau)
        return mag * jnp.cos(w_ * tau), mag * jnp.sin(w_ * tau)

    p1r, p1i = lam_pow(a, w, 1.0)
    nr, ni = p1r - 1.0, p1i
    den = lr * lr + li * li
    kr, ki = (nr * lr + ni * li) / den, (ni * lr - nr * li) / den
    br, bi = b_re.astype(f32), b_im.astype(f32)
    bbr = kr[..., None] * br - ki[..., None] * bi
    bbi = kr[..., None] * bi + ki[..., None] * br
    cr, ci = c_re.astype(f32), c_im.astype(f32)
    per, pei = lam_pow(a[..., None], w[..., None], jnp.arange(nl - 1, -1, -1, dtype=f32))
    er = per[..., None] * bbr[:, :, None, :] - pei[..., None] * bbi[:, :, None, :]
    ei = per[..., None] * bbi[:, :, None, :] + pei[..., None] * bbr[:, :, None, :]
    bs = jnp.concatenate([er.reshape(g, npz, lc), ei.reshape(g, npz, lc)], axis=1)
    krev = jnp.einsum('gcq,gqx->gcx', jnp.concatenate([cr, -ci], axis=-1), bs, precision=hi)
    kpad = jnp.concatenate([krev, jnp.zeros((g, nc, lc - nc), f32)], axis=-1)
    tz = jnp.stack([kpad[:, :, (nl - 1 - j) * nc:(2 * nl - 1 - j) * nc] for j in range(nl)], axis=1)
    tz = tz.reshape(g, lc, lc)
    pcr, pci = lam_pow(a[:, None, :], w[:, None, :], jnp.arange(1, nl + 1, dtype=f32)[:, None])
    cpr = cr[:, None] * pcr[:, :, None, :] - ci[:, None] * pci[:, :, None, :]
    cpi = cr[:, None] * pci[:, :, None, :] + ci[:, None] * pcr[:, :, None, :]
    cs = jnp.concatenate([cpr, -cpi], axis=-1).reshape(g, lc, 2 * npz)
    plr, pli = lam_pow(a, w, float(nl))
    l16r = plr.reshape(g // 2, 2 * npz)
    l16i = pli.reshape(g // 2, 2 * npz)
    return _bf(tz), _bf(bs), _bf(cs), l16r, l16i


def _sigmoid_tanh(x):
    return 0.5 + 0.5 * jnp.tanh(0.5 * x)


def _glu_kernel(a_ref, w_ref, gt_ref, b_ref, o_ref, z_scr):
    nct, tm, tn = a_ref.shape
    j = pl.program_id(1)

    hm = tm // 2
    for r0 in (0, hm):
        rows = slice(r0, r0 + hm)
        acc = jnp.dot(a_ref[0, rows, :], w_ref[:tn, :], preferred_element_type=jnp.float32)
        for c in range(1, nct):
            acc += jnp.dot(a_ref[c, rows, :], w_ref[c * tn:(c + 1) * tn, :], preferred_element_type=jnp.float32)
        z_scr[rows, :] = acc + b_ref[...]
        g = gt_ref[rows, :]
        gate = _sigmoid_tanh(z_scr[rows, :]) * (g * _sigmoid_tanh(g))
        o_ref[0, rows, :] = _bf(a_ref[j, rows, :].astype(jnp.float32) * gate)


def _glu(y_tiles, w, bias, proj, tm):
    nct, t, tn = y_tiles.shape
    e = nct * tn
    half = e // 2
    nh = half // tn
    goff = e // tn
    return pl.pallas_call(
        _glu_kernel,
        grid=(t // tm, e // tn),
        in_specs=[pl.BlockSpec((nct, tm, tn), lambda i, j: (0, i, 0)),
                  pl.BlockSpec((e, tn), lambda i, j: (0, j)),
                  pl.BlockSpec((tm, tn), lambda i, j: (i, goff + j)),
                  pl.BlockSpec((1, tn), lambda i, j: (0, j))],
        out_specs=pl.BlockSpec((1, tm, tn), lambda i, j: (j // nh, i, j % nh)),
        out_shape=jax.ShapeDtypeStruct((2, t, half), jnp.bfloat16),
        scratch_shapes=[pltpu.VMEM((tm, tn), jnp.float32)],
        compiler_params=pltpu.CompilerParams(
            dimension_semantics=("parallel", "arbitrary"), vmem_limit_bytes=VMEM_LIMIT),
        name="glu",
    )(y_tiles, w, proj, bias.reshape(1, e))


def kernel(x, ev_w_in, ev_conv_w, ev_hg_norm, ev_w_out, ev_ln_g, ev_ln_b, hg_lb_logits, od_w_in, od_lam_re,
           od_lam_im, od_log_step, od_b_re, od_b_im, od_c_re, od_c_im, od_d, od_w_glu, od_b_glu, od_w_out,
           od_ln_g, od_ln_b):
    bsz, seq, d = x.shape
    t = bsz * seq
    f32 = jnp.float32
    h0 = x.reshape(t, d).astype(f32)

    lb_all = jnp.cumsum(jax.nn.softmax(hg_lb_logits.astype(f32), axis=0), axis=0)

    proj0 = _matmul(h0, _bf(ev_w_in[0]), f32, tm=512, tn=1024)
    y0 = _mixer0(proj0, ev_conv_w[0].astype(f32), lb_all[0], ev_hg_norm[0].astype(f32), bsz, seq)
    h1, h1_bf = _out_ln(y0, _bf(ev_w_out[0]), h0, ev_ln_g[0].astype(f32), ev_ln_b[0].astype(f32),
                        tm=256, tn=512, with_bf16_copy=True)

    proj1 = _matmul(h1_bf, _bf(od_w_in[0]), f32, tm=1024, tn=1024)
    tz, bs, cs, l16r, l16i = _s5_operators(od_lam_re[0], od_lam_im[0], od_log_step[0], od_b_re[0], od_b_im[0],
                                           od_c_re[0], od_c_im[0])
    ys = _s5(proj1, tz, bs, cs, l16r, l16i, od_d[0].astype(f32), bsz, seq)
    y1 = _glu(ys, _bf(od_w_glu[0]), od_b_glu[0].astype(f32), proj1, tm=512)
    h2, = _out_ln(y1, _bf(od_w_out[0]), h1, od_ln_g[0].astype(f32), od_ln_b[0].astype(f32),
                  tm=256, tn=512, with_bf16_copy=False)
    return h2.reshape(bsz, seq, d).astype(x.dtype)
```

```python
import functools
import math

import jax
import jax.numpy as jnp
from jax import lax
from jax.experimental import pallas as pl
from jax.experimental.pallas import tpu as pltpu

DEPTH = 2
ALPHA = (2 * DEPTH) ** 0.25
LN_EPS = 1e-5
RMS_EPS = 1e-6
LAMBDA_RE_MAX = -1e-4
CONV_K = 3
HG_HEADS = 16
HG_CHUNK = 32
S5_GROUP = 16
S5_STATE = 64

LANES = 128
SUBLANES = 8
VMEM_LIMIT = 61 * 1024 * 1024

MIX_TB = 256
MIX_HB = 16
S5_L = 16
S5_R = 128
S5_GB = 16
S5_RP = S5_R + SUBLANES
GLU_TN = 1024
GLU_ROWS = 256

_NT = (((1,), (1,)), ((), ()))


def _bf(x):
    return x.astype(jnp.bfloat16)


def _mm_kernel(a_ref, b_ref, o_ref, *maybe_a_bf):
    if maybe_a_bf:
        a_bf, = maybe_a_bf

        @pl.when(pl.program_id(1) == 0)
        def _():
            a_bf[...] = _bf(a_ref[...])
    else:
        a_bf = a_ref
    o_ref[...] = jnp.dot(a_bf[...], b_ref[...], preferred_element_type=jnp.float32).astype(o_ref.dtype)


def _matmul(a, b, out_dtype, tm, tn):
    m, k = a.shape
    n = b.shape[1]
    scratch = [] if a.dtype == jnp.bfloat16 else [pltpu.VMEM((tm, k), jnp.bfloat16)]
    return pl.pallas_call(
        _mm_kernel,
        grid=(m // tm, n // tn),
        in_specs=[pl.BlockSpec((tm, k), lambda i, j: (i, 0)),
                  pl.BlockSpec((k, tn), lambda i, j: (0, j))],
        out_specs=pl.BlockSpec((tm, tn), lambda i, j: (i, j)),
        out_shape=jax.ShapeDtypeStruct((m, n), out_dtype),
        scratch_shapes=scratch,
        compiler_params=pltpu.CompilerParams(
            dimension_semantics=("parallel", "arbitrary"), vmem_limit_bytes=VMEM_LIMIT),
        name="in_proj",
    )(a, b)


LN_ROWS = 4 * SUBLANES


def _out_ln_kernel(y_ref, w_ref, r_ref, g_ref, b_ref, o_ref, *maybe_obf, tn):
    eh = y_ref.shape[2]
    tm, d = o_ref.shape
    for c in range(d // tn):
        cols = slice(c * tn, (c + 1) * tn)
        acc = jnp.dot(y_ref[0], w_ref[:eh, cols], preferred_element_type=jnp.float32)
        acc += jnp.dot(y_ref[1], w_ref[eh:, cols], preferred_element_type=jnp.float32)
        o_ref[:, cols] = ALPHA * r_ref[:, cols] + acc

    def ln_slab(r0):
        for rb in range(LN_ROWS // SUBLANES):
            rows = slice(r0 + rb * SUBLANES, r0 + (rb + 1) * SUBLANES)
            z = o_ref[rows, :]
            mu = jnp.mean(z, axis=-1, keepdims=True)
            zc = z - mu
            var = jnp.mean(zc * zc, axis=-1, keepdims=True)
            o_ref[rows, :] = zc * lax.rsqrt(var + LN_EPS) * g_ref[...] + b_ref[...]
        if maybe_obf:
            maybe_obf[0][r0:r0 + LN_ROWS, :] = _bf(o_ref[r0:r0 + LN_ROWS, :])

    for r0 in range(0, tm, LN_ROWS):
        pl.when(pl.program_id(0) >= 0)(functools.partial(ln_slab, r0))


def _out_ln(y2, w, resid, g, b, tm, tn, with_bf16_copy):
    _, t, eh = y2.shape
    e, d = w.shape
    row_spec = pl.BlockSpec((tm, d), lambda i: (i, 0))
    out_shape = [jax.ShapeDtypeStruct((t, d), jnp.float32)]
    if with_bf16_copy:
        out_shape.append(jax.ShapeDtypeStruct((t, d), jnp.bfloat16))
    return pl.pallas_call(
        functools.partial(_out_ln_kernel, tn=tn),
        grid=(t // tm,),
        in_specs=[pl.BlockSpec((2, tm, eh), lambda i: (0, i, 0)),
                  pl.BlockSpec((e, d), lambda i: (0, 0), pipeline_mode=pl.Buffered(1)),
                  row_spec,
                  pl.BlockSpec((1, d), lambda i: (0, 0)),
                  pl.BlockSpec((1, d), lambda i: (0, 0))],
        out_specs=[row_spec] * len(out_shape),
        out_shape=out_shape,
        compiler_params=pltpu.CompilerParams(
            dimension_semantics=("arbitrary",), vmem_limit_bytes=VMEM_LIMIT),
        name="out_proj_ln",
    )(y2, w, resid, g.reshape(1, d), b.reshape(1, d))


def _chunk_cumsum(x, row_in_chunk):
    sh = 1
    while sh < HG_CHUNK:
        x = x + jnp.where(row_in_chunk >= sh, pltpu.roll(x, sh, 0), 0.0)
        sh *= 2
    return x


def _mixer0_kernel(ab_ref, ac_ref, ah_ref, q_ref, f_ref, v_ref, ga_ref, gb_ref,
                   cw_ref, lb_ref, nw_ref, o_ref, carry, state):
    tb = pl.program_id(2)
    tbk, w = ab_ref.shape
    nc = tbk // HG_CHUNK

    @pl.when(tb == 0)
    def _():
        carry[...] = jnp.zeros_like(carry)
        state[...] = jnp.zeros_like(state)

    p = ac_ref[...] * ah_ref[...]
    row = lax.broadcasted_iota(jnp.int32, (tbk, w), 0)
    c6 = carry[SUBLANES - 2:SUBLANES - 1, :]
    c7 = carry[SUBLANES - 1:SUBLANES, :]
    p1 = jnp.where(row == 0, c7, pltpu.roll(p, 1, 0))
    p2 = jnp.where(row == 0, c6, jnp.where(row == 1, c7, pltpu.roll(p, 2, 0)))
    conv = cw_ref[0:1, :] * p2 + cw_ref[1:2, :] * p1 + cw_ref[2:3, :] * p
    carry[...] = p[tbk - SUBLANES:, :]
    o_ref[0] = _bf(ab_ref[...] * conv * jax.nn.silu(ga_ref[...]))

    rowh = lax.broadcasted_iota(jnp.int32, (tbk, LANES), 0)
    ric = rowh % HG_CHUNK
    r2 = lax.broadcasted_iota(jnp.int32, (tbk, tbk), 0)
    c2 = lax.broadcasted_iota(jnp.int32, (tbk, tbk), 1)
    causal = (r2 // HG_CHUNK == c2 // HG_CHUNK) & (c2 <= r2)
    lane_chunk = lax.broadcasted_iota(jnp.int32, (LANES, tbk), 1) // HG_CHUNK
    outs = []
    for h in range(w // LANES):
        sl = slice(h * LANES, (h + 1) * LANES)
        q = q_ref[:, sl]
        v = v_ref[:, sl]
        lb = lb_ref[:, sl]
        f = lb + (1.0 - lb) * jax.nn.sigmoid(f_ref[:, sl])
        kk = 1.0 - f
        b = _chunk_cumsum(jnp.log(f), ric)
        b_last_rows = [b[(c + 1) * HG_CHUNK - 1:(c + 1) * HG_CHUNK, :] for c in range(nc)]
        b_last = jnp.concatenate([jnp.broadcast_to(r, (HG_CHUNK, LANES)) for r in b_last_rows], axis=0)
        q_in = _bf(q * jnp.exp(b))
        k_in = _bf(kk * jnp.exp(-b))
        k_dec = _bf(kk * jnp.exp(b_last - b))
        vb = _bf(v)
        s = lax.dot_general(q_in, k_in, _NT, preferred_element_type=jnp.float32)
        s = _bf(jnp.where(causal, s, 0.0))
        o_intra = jnp.dot(s, vb, preferred_element_type=jnp.float32)
        v_t = _bf(v.T)
        zero_t = jnp.zeros_like(v_t)
        v_stack = jnp.concatenate([jnp.where(lane_chunk == c, v_t, zero_t) for c in range(nc)], axis=0)
        upd = jnp.dot(v_stack, k_dec, preferred_element_type=jnp.float32)
        st = state[h]
        sts = []
        for c in range(nc):
            sts.append(_bf(st))
            st = st * jnp.exp(b_last_rows[c]) + upd[c * LANES:(c + 1) * LANES]
        state[h] = st
        st_all = jnp.concatenate(sts, axis=1)
        zero_q = jnp.zeros_like(q_in)
        q_exp = jnp.concatenate([jnp.where(rowh // HG_CHUNK == c, q_in, zero_q) for c in range(nc)], axis=1)
        o_inter = lax.dot_general(q_exp, st_all, _NT, preferred_element_type=jnp.float32)
        o = o_intra + o_inter
        o = o * lax.rsqrt(jnp.mean(o * o, axis=-1, keepdims=True) + RMS_EPS) * nw_ref[...]
        outs.append(o)
    ob = outs[0] if len(outs) == 1 else jnp.concatenate(outs, axis=1)
    o_ref[1] = _bf(ob * jax.nn.silu(gb_ref[...]))


def _mixer0(proj, conv_w, lb, hg_norm, bsz, seq):
    t, e_in = proj.shape
    cw = conv_w.shape[1]
    w = MIX_HB * LANES
    nsec = cw // w
    nt = seq // MIX_TB

    def sec(k):
        return pl.BlockSpec((MIX_TB, w), lambda b, g, s, k=k: (b * nt + s, k * nsec + g))

    return pl.pallas_call(
        _mixer0_kernel,
        grid=(bsz, nsec, nt),
        in_specs=[sec(0), sec(1), sec(2), sec(3), sec(4), sec(5), sec(6), sec(7),
                  pl.BlockSpec((CONV_K, w), lambda b, g, s: (0, g)),
                  pl.BlockSpec((1, w), lambda b, g, s: (0, g)),
                  pl.BlockSpec((1, LANES), lambda b, g, s: (0, 0))],
        out_specs=pl.BlockSpec((2, MIX_TB, w), lambda b, g, s: (0, b * nt + s, g)),
        out_shape=jax.ShapeDtypeStruct((2, t, cw), jnp.bfloat16),
        scratch_shapes=[pltpu.VMEM((SUBLANES, w), jnp.float32),
                        pltpu.VMEM((MIX_HB, LANES, LANES), jnp.float32)],
        compiler_params=pltpu.CompilerParams(
            dimension_semantics=("parallel", "parallel", "arbitrary"), vmem_limit_bytes=VMEM_LIMIT),
        name="mixer0",
    )(proj, proj, proj, proj, proj, proj, proj, proj, conv_w, lb.reshape(1, cw), hg_norm.reshape(1, LANES))


def _gelu_tanh(x):
    return 0.5 * x * (1.0 + jnp.tanh(math.sqrt(2.0 / math.pi) * (x + 0.044715 * (x * x * x))))


def _s5_kernel(*refs, nseq, nh):
    u_refs = refs[:nseq * nh]
    tz_ref, bs_ref, cs_ref, lr_ref, li_ref, d_ref, o_ref = refs[nseq * nh:nseq * nh + 7]
    scr = refs[nseq * nh + 7:]
    x_scr, y_scr, st_r, st_i = scr[:4]
    ar_scrs, ai_scrs, sr_scrs, si_scrs = (scr[4 + q * nseq:4 + (q + 1) * nseq] for q in range(4))
    o_scrs = scr[4 + 4 * nseq:]
    gb = tz_ref.shape[0]
    r = S5_R
    rp = S5_RP
    npair = gb // 2
    gph = LANES // S5_GROUP
    p = S5_STATE

    @pl.when(pl.program_id(1) == 0)
    def _():
        st_r[...] = jnp.zeros_like(st_r)
        st_i[...] = jnp.zeros_like(st_i)

    for b in range(nseq):
        for j in range(S5_L):
            for h in range(nh):
                rows = u_refs[b * nh + h][pl.ds(j, r, stride=S5_L), :]
                x_scr[b * gb + h * gph:b * gb + (h + 1) * gph, j * S5_GROUP:(j + 1) * S5_GROUP, :] = (
                    _bf(rows.T).reshape(gph, S5_GROUP, r))

    for b in range(nseq):
        for k in range(npair):
            a = []
            for g in (2 * k, 2 * k + 1):
                xg = x_scr[b * gb + g]
                y_scr[b * gb + g] = jnp.dot(tz_ref[g], xg, preferred_element_type=jnp.float32)
                a.append(jnp.dot(bs_ref[g], xg, preferred_element_type=jnp.float32))
            ar_scrs[b][k * rp:k * rp + r, :] = jnp.concatenate([a[0][:p], a[1][:p]], axis=0).T
            ai_scrs[b][k * rp:k * rp + r, :] = jnp.concatenate([a[0][p:], a[1][p:]], axis=0).T

    lr, li = lr_ref[...], li_ref[...]
    srs = [st_r[b] for b in range(nseq)]
    sis = [st_i[b] for b in range(nseq)]
    for n in range(r):
        for b in range(nseq):
            sr, si = srs[b], sis[b]
            sr_scrs[b][pl.ds(n, npair, stride=rp), :] = sr
            si_scrs[b][pl.ds(n, npair, stride=rp), :] = si
            a_r = ar_scrs[b][pl.ds(n, npair, stride=rp), :]
            a_i = ai_scrs[b][pl.ds(n, npair, stride=rp), :]
            srs[b], sis[b] = sr * lr - si * li + a_r, sr * li + si * lr + a_i
    for b in range(nseq):
        st_r[b] = srs[b]
        st_i[b] = sis[b]

    for b in range(nseq):
        for k in range(npair):
            srt = sr_scrs[b][k * rp:k * rp + r, :].T
            sit = si_scrs[b][k * rp:k * rp + r, :].T
            s0 = _bf(jnp.concatenate([srt[:p], sit[:p]], axis=0))
            s1 = _bf(jnp.concatenate([srt[p:], sit[p:]], axis=0))
            y_scr[b * gb + 2 * k] += jnp.dot(cs_ref[2 * k], s0, preferred_element_type=jnp.float32)
            y_scr[b * gb + 2 * k + 1] += jnp.dot(cs_ref[2 * k + 1], s1, preferred_element_type=jnp.float32)

    for b in range(nseq):
        for j in range(S5_L):
            for h in range(nh):
                yj = y_scr[b * gb + h * gph:b * gb + (h + 1) * gph, j * S5_GROUP:(j + 1) * S5_GROUP, :]
                yj = yj.reshape(LANES, r).T
                uj = u_refs[b * nh + h][pl.ds(j, r, stride=S5_L), :]
                dsk = d_ref[:, h * LANES:(h + 1) * LANES]
                o_scrs[b * nh + h][pl.ds(j, r, stride=S5_L), :] = _gelu_tanh(yj + dsk * uj)
        for h in range(nh):
            o_ref[b, :, h * LANES:(h + 1) * LANES] = _bf(o_scrs[b * nh + h][...])


def _s5(proj, tz, bs, cs, l16r, l16i, d_skip, bsz, seq):
    g_total = tz.shape[0]
    width = g_total * S5_GROUP
    ch = S5_GB * S5_GROUP
    nh = ch // LANES
    tile = S5_L * S5_R
    nt = seq // tile
    p2 = 2 * S5_STATE
    lc = S5_L * S5_GROUP
    npair = S5_GB // 2
    kern = functools.partial(_s5_kernel, nseq=bsz, nh=nh)
    u_specs = [pl.BlockSpec((tile, LANES), lambda g, i, b=b, h=h: (b * nt + i, g * nh + h))
               for b in range(bsz) for h in range(nh)]
    state_rows = pltpu.VMEM((npair * S5_RP, p2), jnp.float32)
    gpt = GLU_TN // ch
    out = pl.pallas_call(
        kern,
        grid=(g_total // S5_GB, nt),
        in_specs=u_specs + [
            pl.BlockSpec((S5_GB, lc, lc), lambda g, i: (g, 0, 0)),
            pl.BlockSpec((S5_GB, p2, lc), lambda g, i: (g, 0, 0)),
            pl.BlockSpec((S5_GB, lc, p2), lambda g, i: (g, 0, 0)),
            pl.BlockSpec((npair, p2), lambda g, i: (g, 0)),
            pl.BlockSpec((npair, p2), lambda g, i: (g, 0)),
            pl.BlockSpec((1, ch), lambda g, i: (0, g))],
        out_specs=pl.BlockSpec((None, bsz, tile, ch), lambda g, i: (g // gpt, 0, i, g % gpt)),
        out_shape=jax.ShapeDtypeStruct((width // GLU_TN, bsz, seq, GLU_TN), jnp.bfloat16),
        scratch_shapes=[pltpu.VMEM((bsz * S5_GB, lc, S5_R), jnp.bfloat16),
                        pltpu.VMEM((bsz * S5_GB, lc, S5_R), jnp.float32),
                        pltpu.VMEM((bsz, npair, p2), jnp.float32),
                        pltpu.VMEM((bsz, npair, p2), jnp.float32)]
        + [state_rows for _ in range(4 * bsz)]
        + [pltpu.VMEM((tile, LANES), jnp.float32) for _ in range(bsz * nh)],
        compiler_params=pltpu.CompilerParams(
            dimension_semantics=("parallel", "arbitrary"), vmem_limit_bytes=VMEM_LIMIT),
        name="s5",
    )(*([proj] * (bsz * nh)), tz, bs, cs, l16r, l16i, d_skip.reshape(1, width))
    return out.reshape(width // GLU_TN, bsz * seq, GLU_TN)


def _s5_operators(lam_re, lam_im, log_step, b_re, b_im, c_re, c_im):
    f32 = jnp.float32
    hi = lax.Precision.HIGHEST
    g = lam_re.shape[0]
    lr = jnp.minimum(lam_re.astype(f32), LAMBDA_RE_MAX)
    li = lam_im.astype(f32)
    dt = jnp.exp(log_step.astype(f32))[:, None]
    nl, nc, npz = S5_L, S5_GROUP, S5_STATE
    lc = nl * nc
    a, w = lr * dt, li * dt

    def lam_pow(a_, w_, tau):
        mag = jnp.exp(a_ * tau)
        return mag * jnp.cos(w_ * tau), mag * jnp.sin(w_ * tau)

    p1r, p1i = lam_pow(a, w, 1.0)
    nr, ni = p1r - 1.0, p1i
    den = lr * lr + li * li
    kr, ki = (nr * lr + ni * li) / den, (ni * lr - nr * li) / den
    br, bi = b_re.astype(f32), b_im.astype(f32)
    bbr = kr[..., None] * br - ki[..., None] * bi
    bbi = kr[..., None] * bi + ki[..., None] * br
    cr, ci = c_re.astype(f32), c_im.astype(f32)
    per, pei = lam_pow(a[..., None], w[..., None], jnp.arange(nl - 1, -1, -1, dtype=f32))
    er = per[..., None] * bbr[:, :, None, :] - pei[..., None] * bbi[:, :, None, :]
    ei = per[..., None] * bbi[:, :, None, :] + pei[..., None] * bbr[:, :, None, :]
    bs = jnp.concatenate([er.reshape(g, npz, lc), ei.reshape(g, npz, lc)], axis=1)
    krev = jnp.einsum('gcq,gqx->gcx', jnp.concatenate([cr, -ci], axis=-1), bs, precision=hi)
    kpad = jnp.concatenate([krev, jnp.zeros((g, nc, lc - nc), f32)], axis=-1)
    tz = jnp.stack([kpad[:, :, (nl - 1 - j) * nc:(2 * nl - 1 - j) * nc] for j in range(nl)], axis=1)
    tz = tz.reshape(g, lc, lc)
    pcr, pci = lam_pow(a[:, None, :], w[:, None, :], jnp.arange(1, nl + 1, dtype=f32)[:, None])
    cpr = cr[:, None] * pcr[:, :, None, :] - ci[:, None] * pci[:, :, None, :]
    cpi = cr[:, None] * pci[:, :, None, :] + ci[:, None] * pcr[:, :, None, :]
    cs = jnp.concatenate([cpr, -cpi], axis=-1).reshape(g, lc, 2 * npz)
    plr, pli = lam_pow(a, w, float(nl))
    l16r = plr.reshape(g // 2, 2 * npz)
    l16i = pli.reshape(g // 2, 2 * npz)
    return _bf(tz), _bf(bs), _bf(cs), l16r, l16i


def _sigmoid_tanh(x):
    return 0.5 + 0.5 * jnp.tanh(0.5 * x)


def _glu_kernel(a_ref, w_ref, gt_ref, b_ref, o_ref, z_scr):
    nct, tm, tn = a_ref.shape
    j = pl.program_id(1)

    hm = GLU_ROWS
    for r0 in range(0, tm, hm):
        rows = slice(r0, r0 + hm)
        acc = jnp.dot(a_ref[0, rows, :], w_ref[:tn, :], preferred_element_type=jnp.float32)
        for c in range(1, nct):
            acc += jnp.dot(a_ref[c, rows, :], w_ref[c * tn:(c + 1) * tn, :], preferred_element_type=jnp.float32)
        z_scr[rows, :] = acc + b_ref[...]
        g = gt_ref[rows, :]
        gate = _sigmoid_tanh(z_scr[rows, :]) * (g * _sigmoid_tanh(g))
        o_ref[0, rows, :] = _bf(a_ref[j, rows, :].astype(jnp.float32) * gate)


def _glu(y_tiles, w, bias, proj, tm):
    nct, t, tn = y_tiles.shape
    e = nct * tn
    half = e // 2
    nh = half // tn
    goff = e // tn
    return pl.pallas_call(
        _glu_kernel,
        grid=(t // tm, e // tn),
        in_specs=[pl.BlockSpec((nct, tm, tn), lambda i, j: (0, i, 0)),
                  pl.BlockSpec((e, tn), lambda i, j: (0, j)),
                  pl.BlockSpec((tm, tn), lambda i, j: (i, goff + j)),
                  pl.BlockSpec((1, tn), lambda i, j: (0, j))],
        out_specs=pl.BlockSpec((1, tm, tn), lambda i, j: (j // nh, i, j % nh)),
        out_shape=jax.ShapeDtypeStruct((2, t, half), jnp.bfloat16),
        scratch_shapes=[pltpu.VMEM((tm, tn), jnp.float32)],
        compiler_params=pltpu.CompilerParams(
            dimension_semantics=("parallel", "arbitrary"), vmem_limit_bytes=VMEM_LIMIT),
        name="glu",
    )(y_tiles, w, proj, bias.reshape(1, e))


def kernel(x, ev_w_in, ev_conv_w, ev_hg_norm, ev_w_out, ev_ln_g, ev_ln_b, hg_lb_logits, od_w_in, od_lam_re,
           od_lam_im, od_log_step, od_b_re, od_b_im, od_c_re, od_c_im, od_d, od_w_glu, od_b_glu, od_w_out,
           od_ln_g, od_ln_b):
    bsz, seq, d = x.shape
    t = bsz * seq
    f32 = jnp.float32
    h0 = x.reshape(t, d).astype(f32)

    lb_all = jnp.cumsum(jax.nn.softmax(hg_lb_logits.astype(f32), axis=0), axis=0)

    proj0 = _matmul(h0, _bf(ev_w_in[0]), f32, tm=512, tn=1024)
    y0 = _mixer0(proj0, ev_conv_w[0].astype(f32), lb_all[0], ev_hg_norm[0].astype(f32), bsz, seq)
    h1, h1_bf = _out_ln(y0, _bf(ev_w_out[0]), h0, ev_ln_g[0].astype(f32), ev_ln_b[0].astype(f32),
                        tm=256, tn=512, with_bf16_copy=True)

    proj1 = _matmul(h1_bf, _bf(od_w_in[0]), f32, tm=1024, tn=1024)
    tz, bs, cs, l16r, l16i = _s5_operators(od_lam_re[0], od_lam_im[0], od_log_step[0], od_b_re[0], od_b_im[0],
                                           od_c_re[0], od_c_im[0])
    ys = _s5(proj1, tz, bs, cs, l16r, l16i, od_d[0].astype(f32), bsz, seq)
    y1 = _glu(ys, _bf(od_w_glu[0]), od_b_glu[0].astype(f32), proj1, tm=1024)
    h2, = _out_ln(y1, _bf(od_w_out[0]), h1, od_ln_g[0].astype(f32), od_ln_b[0].astype(f32),
                  tm=256, tn=512, with_bf16_copy=False)
    return h2.reshape(bsz, seq, d).astype(x.dtype)
```

```python
import functools
import math

import jax
import jax.numpy as jnp
from jax import lax
from jax.experimental import pallas as pl
from jax.experimental.pallas import tpu as pltpu

DEPTH = 2
ALPHA = (2 * DEPTH) ** 0.25
LN_EPS = 1e-5
RMS_EPS = 1e-6
LAMBDA_RE_MAX = -1e-4
CONV_K = 3
HG_HEADS = 16
HG_CHUNK = 32
S5_GROUP = 16
S5_STATE = 64

LANES = 128
SUBLANES = 8
VMEM_LIMIT = 61 * 1024 * 1024

MIX_TB = 256
MIX_HB = 16
S5_L = 16
S5_R = 128
S5_GB = 16
S5_RP = S5_R + SUBLANES
S5_LP = S5_L + SUBLANES
GLU_TN = 1024
GLU_ROWS = 256

_NT = (((1,), (1,)), ((), ()))


def _bf(x):
    return x.astype(jnp.bfloat16)


def _mm_kernel(a_ref, b_ref, o_ref, *maybe_a_bf):
    if maybe_a_bf:
        a_bf, = maybe_a_bf

        @pl.when(pl.program_id(1) == 0)
        def _():
            a_bf[...] = _bf(a_ref[...])
    else:
        a_bf = a_ref
    o_ref[...] = jnp.dot(a_bf[...], b_ref[...], preferred_element_type=jnp.float32).astype(o_ref.dtype)


def _matmul(a, b, out_dtype, tm, tn):
    m, k = a.shape
    n = b.shape[1]
    scratch = [] if a.dtype == jnp.bfloat16 else [pltpu.VMEM((tm, k), jnp.bfloat16)]
    return pl.pallas_call(
        _mm_kernel,
        grid=(m // tm, n // tn),
        in_specs=[pl.BlockSpec((tm, k), lambda i, j: (i, 0)),
                  pl.BlockSpec((k, tn), lambda i, j: (0, j))],
        out_specs=pl.BlockSpec((tm, tn), lambda i, j: (i, j)),
        out_shape=jax.ShapeDtypeStruct((m, n), out_dtype),
        scratch_shapes=scratch,
        compiler_params=pltpu.CompilerParams(
            dimension_semantics=("parallel", "arbitrary"), vmem_limit_bytes=VMEM_LIMIT),
        name="in_proj",
    )(a, b)


LN_ROWS = 4 * SUBLANES


def _out_ln_kernel(y_ref, w_ref, r_ref, g_ref, b_ref, o_ref, *maybe_obf, tn):
    eh = y_ref.shape[2]
    tm, d = o_ref.shape
    for c in range(d // tn):
        cols = slice(c * tn, (c + 1) * tn)
        acc = jnp.dot(y_ref[0], w_ref[:eh, cols], preferred_element_type=jnp.float32)
        acc += jnp.dot(y_ref[1], w_ref[eh:, cols], preferred_element_type=jnp.float32)
        o_ref[:, cols] = ALPHA * r_ref[:, cols] + acc

    def ln_slab(r0):
        for rb in range(LN_ROWS // SUBLANES):
            rows = slice(r0 + rb * SUBLANES, r0 + (rb + 1) * SUBLANES)
            z = o_ref[rows, :]
            mu = jnp.mean(z, axis=-1, keepdims=True)
            zc = z - mu
            var = jnp.mean(zc * zc, axis=-1, keepdims=True)
            o_ref[rows, :] = zc * lax.rsqrt(var + LN_EPS) * g_ref[...] + b_ref[...]
        if maybe_obf:
            maybe_obf[0][r0:r0 + LN_ROWS, :] = _bf(o_ref[r0:r0 + LN_ROWS, :])

    for r0 in range(0, tm, LN_ROWS):
        pl.when(pl.program_id(0) >= 0)(functools.partial(ln_slab, r0))


def _out_ln(y2, w, resid, g, b, tm, tn, with_bf16_copy):
    _, t, eh = y2.shape
    e, d = w.shape
    row_spec = pl.BlockSpec((tm, d), lambda i: (i, 0))
    out_shape = [jax.ShapeDtypeStruct((t, d), jnp.float32)]
    if with_bf16_copy:
        out_shape.append(jax.ShapeDtypeStruct((t, d), jnp.bfloat16))
    return pl.pallas_call(
        functools.partial(_out_ln_kernel, tn=tn),
        grid=(t // tm,),
        in_specs=[pl.BlockSpec((2, tm, eh), lambda i: (0, i, 0)),
                  pl.BlockSpec((e, d), lambda i: (0, 0), pipeline_mode=pl.Buffered(1)),
                  row_spec,
                  pl.BlockSpec((1, d), lambda i: (0, 0)),
                  pl.BlockSpec((1, d), lambda i: (0, 0))],
        out_specs=[row_spec] * len(out_shape),
        out_shape=out_shape,
        compiler_params=pltpu.CompilerParams(
            dimension_semantics=("arbitrary",), vmem_limit_bytes=VMEM_LIMIT),
        name="out_proj_ln",
    )(y2, w, resid, g.reshape(1, d), b.reshape(1, d))


def _chunk_cumsum(x, row_in_chunk):
    sh = 1
    while sh < HG_CHUNK:
        x = x + jnp.where(row_in_chunk >= sh, pltpu.roll(x, sh, 0), 0.0)
        sh *= 2
    return x


def _mixer0_kernel(ab_ref, ac_ref, ah_ref, q_ref, f_ref, v_ref, ga_ref, gb_ref,
                   cw_ref, lb_ref, nw_ref, o_ref, carry, state):
    tb = pl.program_id(2)
    tbk, w = ab_ref.shape
    nc = tbk // HG_CHUNK

    @pl.when(tb == 0)
    def _():
        carry[...] = jnp.zeros_like(carry)
        state[...] = jnp.zeros_like(state)

    p = ac_ref[...] * ah_ref[...]
    row = lax.broadcasted_iota(jnp.int32, (tbk, w), 0)
    c6 = carry[SUBLANES - 2:SUBLANES - 1, :]
    c7 = carry[SUBLANES - 1:SUBLANES, :]
    p1 = jnp.where(row == 0, c7, pltpu.roll(p, 1, 0))
    p2 = jnp.where(row == 0, c6, jnp.where(row == 1, c7, pltpu.roll(p, 2, 0)))
    conv = cw_ref[0:1, :] * p2 + cw_ref[1:2, :] * p1 + cw_ref[2:3, :] * p
    carry[...] = p[tbk - SUBLANES:, :]
    o_ref[0] = _bf(ab_ref[...] * conv * jax.nn.silu(ga_ref[...]))

    rowh = lax.broadcasted_iota(jnp.int32, (tbk, LANES), 0)
    ric = rowh % HG_CHUNK
    r2 = lax.broadcasted_iota(jnp.int32, (tbk, tbk), 0)
    c2 = lax.broadcasted_iota(jnp.int32, (tbk, tbk), 1)
    causal = (r2 // HG_CHUNK == c2 // HG_CHUNK) & (c2 <= r2)
    lane_chunk = lax.broadcasted_iota(jnp.int32, (LANES, tbk), 1) // HG_CHUNK
    outs = []
    for h in range(w // LANES):
        sl = slice(h * LANES, (h + 1) * LANES)
        q = q_ref[:, sl]
        v = v_ref[:, sl]
        lb = lb_ref[:, sl]
        f = lb + (1.0 - lb) * jax.nn.sigmoid(f_ref[:, sl])
        kk = 1.0 - f
        b = _chunk_cumsum(jnp.log(f), ric)
        b_last_rows = [b[(c + 1) * HG_CHUNK - 1:(c + 1) * HG_CHUNK, :] for c in range(nc)]
        b_last = jnp.concatenate([jnp.broadcast_to(r, (HG_CHUNK, LANES)) for r in b_last_rows], axis=0)
        q_in = _bf(q * jnp.exp(b))
        k_in = _bf(kk * jnp.exp(-b))
        k_dec = _bf(kk * jnp.exp(b_last - b))
        vb = _bf(v)
        s = lax.dot_general(q_in, k_in, _NT, preferred_element_type=jnp.float32)
        s = _bf(jnp.where(causal, s, 0.0))
        o_intra = jnp.dot(s, vb, preferred_element_type=jnp.float32)
        v_t = _bf(v.T)
        zero_t = jnp.zeros_like(v_t)
        v_stack = jnp.concatenate([jnp.where(lane_chunk == c, v_t, zero_t) for c in range(nc)], axis=0)
        upd = jnp.dot(v_stack, k_dec, preferred_element_type=jnp.float32)
        st = state[h]
        sts = []
        for c in range(nc):
            sts.append(_bf(st))
            st = st * jnp.exp(b_last_rows[c]) + upd[c * LANES:(c + 1) * LANES]
        state[h] = st
        st_all = jnp.concatenate(sts, axis=1)
        zero_q = jnp.zeros_like(q_in)
        q_exp = jnp.concatenate([jnp.where(rowh // HG_CHUNK == c, q_in, zero_q) for c in range(nc)], axis=1)
        o_inter = lax.dot_general(q_exp, st_all, _NT, preferred_element_type=jnp.float32)
        o = o_intra + o_inter
        o = o * lax.rsqrt(jnp.mean(o * o, axis=-1, keepdims=True) + RMS_EPS) * nw_ref[...]
        outs.append(o)
    ob = outs[0] if len(outs) == 1 else jnp.concatenate(outs, axis=1)
    o_ref[1] = _bf(ob * jax.nn.silu(gb_ref[...]))


def _mixer0(proj, conv_w, lb, hg_norm, bsz, seq):
    t, e_in = proj.shape
    cw = conv_w.shape[1]
    w = MIX_HB * LANES
    nsec = cw // w
    nt = seq // MIX_TB

    def sec(k):
        return pl.BlockSpec((MIX_TB, w), lambda b, g, s, k=k: (b * nt + s, k * nsec + g))

    return pl.pallas_call(
        _mixer0_kernel,
        grid=(bsz, nsec, nt),
        in_specs=[sec(0), sec(1), sec(2), sec(3), sec(4), sec(5), sec(6), sec(7),
                  pl.BlockSpec((CONV_K, w), lambda b, g, s: (0, g)),
                  pl.BlockSpec((1, w), lambda b, g, s: (0, g)),
                  pl.BlockSpec((1, LANES), lambda b, g, s: (0, 0))],
        out_specs=pl.BlockSpec((2, MIX_TB, w), lambda b, g, s: (0, b * nt + s, g)),
        out_shape=jax.ShapeDtypeStruct((2, t, cw), jnp.bfloat16),
        scratch_shapes=[pltpu.VMEM((SUBLANES, w), jnp.float32),
                        pltpu.VMEM((MIX_HB, LANES, LANES), jnp.float32)],
        compiler_params=pltpu.CompilerParams(
            dimension_semantics=("parallel", "parallel", "arbitrary"), vmem_limit_bytes=VMEM_LIMIT),
        name="mixer0",
    )(proj, proj, proj, proj, proj, proj, proj, proj, conv_w, lb.reshape(1, cw), hg_norm.reshape(1, LANES))


def _gelu_tanh(x):
    return 0.5 * x * (1.0 + jnp.tanh(math.sqrt(2.0 / math.pi) * (x + 0.044715 * (x * x * x))))


def _s5_kernel(*refs, nseq, nh):
    u_refs = refs[:nseq * nh]
    tz_ref, bs_ref, cs_ref, lr_ref, li_ref, d_ref, o_ref = refs[nseq * nh:nseq * nh + 7]
    scr = refs[nseq * nh + 7:]
    x_scr, y_scr, st_r, st_i = scr[:4]
    ar_scrs, ai_scrs, sr_scrs, si_scrs = (scr[4 + q * nseq:4 + (q + 1) * nseq] for q in range(4))
    o_scrs = scr[4 + 4 * nseq:]
    gb = tz_ref.shape[0]
    r = S5_R
    rp = S5_RP
    npair = gb // 2
    gph = LANES // S5_GROUP
    p = S5_STATE

    @pl.when(pl.program_id(1) == 0)
    def _():
        st_r[...] = jnp.zeros_like(st_r)
        st_i[...] = jnp.zeros_like(st_i)
        for o_scr in o_scrs:
            o_scr[...] = jnp.zeros_like(o_scr)

    for b in range(nseq):
        for j in range(S5_L):
            for h in range(nh):
                rows = u_refs[b * nh + h][pl.ds(j, r, stride=S5_L), :]
                x_scr[b * gb + h * gph:b * gb + (h + 1) * gph, j * S5_GROUP:(j + 1) * S5_GROUP, :] = (
                    _bf(rows.T).reshape(gph, S5_GROUP, r))

    for b in range(nseq):
        for k in range(npair):
            a = [jnp.dot(bs_ref[g], x_scr[b * gb + g], preferred_element_type=jnp.float32)
                 for g in (2 * k, 2 * k + 1)]
            ar_scrs[b][k * rp:k * rp + r, :] = jnp.concatenate([a[0][:p], a[1][:p]], axis=0).T
            ai_scrs[b][k * rp:k * rp + r, :] = jnp.concatenate([a[0][p:], a[1][p:]], axis=0).T

    lr, li = lr_ref[...], li_ref[...]
    srs = [st_r[b] for b in range(nseq)]
    sis = [st_i[b] for b in range(nseq)]
    for n in range(r):
        for b in range(nseq):
            sr, si = srs[b], sis[b]
            sr_scrs[b][pl.ds(n, npair, stride=rp), :] = sr
            si_scrs[b][pl.ds(n, npair, stride=rp), :] = si
            a_r = ar_scrs[b][pl.ds(n, npair, stride=rp), :]
            a_i = ai_scrs[b][pl.ds(n, npair, stride=rp), :]
            srs[b], sis[b] = sr * lr - si * li + a_r, sr * li + si * lr + a_i
    for b in range(nseq):
        st_r[b] = srs[b]
        st_i[b] = sis[b]

    for b in range(nseq):
        for k in range(npair):
            srt = sr_scrs[b][k * rp:k * rp + r, :].T
            sit = si_scrs[b][k * rp:k * rp + r, :].T
            s_in = (_bf(jnp.concatenate([srt[:p], sit[:p]], axis=0)),
                    _bf(jnp.concatenate([srt[p:], sit[p:]], axis=0)))
            for q, g in enumerate((2 * k, 2 * k + 1)):
                y_scr[b * gb + g] = (jnp.dot(tz_ref[g], x_scr[b * gb + g], preferred_element_type=jnp.float32)
                                     + jnp.dot(cs_ref[g], s_in[q], preferred_element_type=jnp.float32))

    for b in range(nseq):
        for j in range(S5_L):
            for h in range(nh):
                yj = y_scr[b * gb + h * gph:b * gb + (h + 1) * gph, j * S5_GROUP:(j + 1) * S5_GROUP, :]
                yj = yj.reshape(LANES, r).T
                uj = u_refs[b * nh + h][pl.ds(j, r, stride=S5_L), :]
                dsk = d_ref[:, h * LANES:(h + 1) * LANES]
                o_scrs[b * nh + h][pl.ds(j, r, stride=S5_LP), :] = _gelu_tanh(yj + dsk * uj)
        for h in range(nh):
            rows = o_scrs[b * nh + h][...].reshape(r, S5_LP, LANES)[:, :S5_L, :]
            o_ref[b, :, h * LANES:(h + 1) * LANES] = _bf(rows.reshape(r * S5_L, LANES))


def _s5(proj, tz, bs, cs, l16r, l16i, d_skip, bsz, seq):
    g_total = tz.shape[0]
    width = g_total * S5_GROUP
    ch = S5_GB * S5_GROUP
    nh = ch // LANES
    tile = S5_L * S5_R
    nt = seq // tile
    p2 = 2 * S5_STATE
    lc = S5_L * S5_GROUP
    npair = S5_GB // 2
    kern = functools.partial(_s5_kernel, nseq=bsz, nh=nh)
    u_specs = [pl.BlockSpec((tile, LANES), lambda g, i, b=b, h=h: (b * nt + i, g * nh + h))
               for b in range(bsz) for h in range(nh)]
    state_rows = pltpu.VMEM((npair * S5_RP, p2), jnp.float32)
    gpt = GLU_TN // ch
    out = pl.pallas_call(
        kern,
        grid=(g_total // S5_GB, nt),
        in_specs=u_specs + [
            pl.BlockSpec((S5_GB, lc, lc), lambda g, i: (g, 0, 0)),
            pl.BlockSpec((S5_GB, p2, lc), lambda g, i: (g, 0, 0)),
            pl.BlockSpec((S5_GB, lc, p2), lambda g, i: (g, 0, 0)),
            pl.BlockSpec((npair, p2), lambda g, i: (g, 0)),
            pl.BlockSpec((npair, p2), lambda g, i: (g, 0)),
            pl.BlockSpec((1, ch), lambda g, i: (0, g))],
        out_specs=pl.BlockSpec((None, bsz, tile, ch), lambda g, i: (g // gpt, 0, i, g % gpt)),
        out_shape=jax.ShapeDtypeStruct((width // GLU_TN, bsz, seq, GLU_TN), jnp.bfloat16),
        scratch_shapes=[pltpu.VMEM((bsz * S5_GB, lc, S5_R), jnp.bfloat16),
                        pltpu.VMEM((bsz * S5_GB, lc, S5_R), jnp.float32),
                        pltpu.VMEM((bsz, npair, p2), jnp.float32),
                        pltpu.VMEM((bsz, npair, p2), jnp.float32)]
        + [state_rows for _ in range(4 * bsz)]
        + [pltpu.VMEM((S5_R * S5_LP, LANES), jnp.float32) for _ in range(bsz * nh)],
        compiler_params=pltpu.CompilerParams(
            dimension_semantics=("parallel", "arbitrary"), vmem_limit_bytes=VMEM_LIMIT),
        name="s5",
    )(*([proj] * (bsz * nh)), tz, bs, cs, l16r, l16i, d_skip.reshape(1, width))
    return out.reshape(width // GLU_TN, bsz * seq, GLU_TN)


def _s5_operators(lam_re, lam_im, log_step, b_re, b_im, c_re, c_im):
    f32 = jnp.float32
    hi = lax.Precision.HIGHEST
    g = lam_re.shape[0]
    lr = jnp.minimum(lam_re.astype(f32), LAMBDA_RE_MAX)
    li = lam_im.astype(f32)
    dt = jnp.exp(log_step.astype(f32))[:, None]
    nl, nc, npz = S5_L, S5_GROUP, S5_STATE
    lc = nl * nc
    a, w = lr * dt, li * dt

    def lam_pow(a_, w_, tau):
        mag = jnp.exp(a_ * tau)
        return mag * jnp.cos(w_ * tau), mag * jnp.sin(w_ * tau)

    p1r, p1i = lam_pow(a, w, 1.0)
    nr, ni = p1r - 1.0, p1i
    den = lr * lr + li * li
    kr, ki = (nr * lr + ni * li) / den, (ni * lr - nr * li) / den
    br, bi = b_re.astype(f32), b_im.astype(f32)
    bbr = kr[..., None] * br - ki[..., None] * bi
    bbi = kr[..., None] * bi + ki[..., None] * br
    cr, ci = c_re.astype(f32), c_im.astype(f32)
    per, pei = lam_pow(a[..., None], w[..., None], jnp.arange(nl - 1, -1, -1, dtype=f32))
    er = per[..., None] * bbr[:, :, None, :] - pei[..., None] * bbi[:, :, None, :]
    ei = per[..., None] * bbi[:, :, None, :] + pei[..., None] * bbr[:, :, None, :]
    bs = jnp.concatenate([er.reshape(g, npz, lc), ei.reshape(g, npz, lc)], axis=1)
    krev = jnp.einsum('gcq,gqx->gcx', jnp.concatenate([cr, -ci], axis=-1), bs, precision=hi)
    kpad = jnp.concatenate([krev, jnp.zeros((g, nc, lc - nc), f32)], axis=-1)
    tz = jnp.stack([kpad[:, :, (nl - 1 - j) * nc:(2 * nl - 1 - j) * nc] for j in range(nl)], axis=1)
    tz = tz.reshape(g, lc, lc)
    pcr, pci = lam_pow(a[:, None, :], w[:, None, :], jnp.arange(1, nl + 1, dtype=f32)[:, None])
    cpr = cr[:, None] * pcr[:, :, None, :] - ci[:, None] * pci[:, :, None, :]
    cpi = cr[:, None] * pci[:, :, None, :] + ci[:, None] * pcr[:, :, None, :]
    cs = jnp.concatenate([cpr, -cpi], axis=-1).reshape(g, lc, 2 * npz)
    plr, pli = lam_pow(a, w, float(nl))
    l16r = plr.reshape(g // 2, 2 * npz)
    l16i = pli.reshape(g // 2, 2 * npz)
    return _bf(tz), _bf(bs), _bf(cs), l16r, l16i


def _sigmoid_tanh(x):
    return 0.5 + 0.5 * jnp.tanh(0.5 * x)


def _glu_kernel(a_ref, w_ref, gt_ref, b_ref, o_ref, z_scr):
    nct, tm, tn = a_ref.shape
    j = pl.program_id(1)

    hm = GLU_ROWS
    for r0 in range(0, tm, hm):
        rows = slice(r0, r0 + hm)
        acc = jnp.dot(a_ref[0, rows, :], w_ref[:tn, :], preferred_element_type=jnp.float32)
        for c in range(1, nct):
            acc += jnp.dot(a_ref[c, rows, :], w_ref[c * tn:(c + 1) * tn, :], preferred_element_type=jnp.float32)
        z_scr[rows, :] = acc + b_ref[...]
        g = gt_ref[rows, :]
        gate = _sigmoid_tanh(z_scr[rows, :]) * (g * _sigmoid_tanh(g))
        o_ref[0, rows, :] = _bf(a_ref[j, rows, :].astype(jnp.float32) * gate)


def _glu(y_tiles, w, bias, proj, tm):
    nct, t, tn = y_tiles.shape
    e = nct * tn
    half = e // 2
    nh = half // tn
    goff = e // tn
    return pl.pallas_call(
        _glu_kernel,
        grid=(t // tm, e // tn),
        in_specs=[pl.BlockSpec((nct, tm, tn), lambda i, j: (0, i, 0)),
                  pl.BlockSpec((e, tn), lambda i, j: (0, j)),
                  pl.BlockSpec((tm, tn), lambda i, j: (i, goff + j)),
                  pl.BlockSpec((1, tn), lambda i, j: (0, j))],
        out_specs=pl.BlockSpec((1, tm, tn), lambda i, j: (j // nh, i, j % nh)),
        out_shape=jax.ShapeDtypeStruct((2, t, half), jnp.bfloat16),
        scratch_shapes=[pltpu.VMEM((tm, tn), jnp.float32)],
        compiler_params=pltpu.CompilerParams(
            dimension_semantics=("parallel", "arbitrary"), vmem_limit_bytes=VMEM_LIMIT),
        name="glu",
    )(y_tiles, w, proj, bias.reshape(1, e))


def kernel(x, ev_w_in, ev_conv_w, ev_hg_norm, ev_w_out, ev_ln_g, ev_ln_b, hg_lb_logits, od_w_in, od_lam_re,
           od_lam_im, od_log_step, od_b_re, od_b_im, od_c_re, od_c_im, od_d, od_w_glu, od_b_glu, od_w_out,
           od_ln_g, od_ln_b):
    bsz, seq, d = x.shape
    t = bsz * seq
    f32 = jnp.float32
    h0 = x.reshape(t, d).astype(f32)

    lb_all = jnp.cumsum(jax.nn.softmax(hg_lb_logits.astype(f32), axis=0), axis=0)

    proj0 = _matmul(h0, _bf(ev_w_in[0]), f32, tm=512, tn=1024)
    y0 = _mixer0(proj0, ev_conv_w[0].astype(f32), lb_all[0], ev_hg_norm[0].astype(f32), bsz, seq)
    h1, h1_bf = _out_ln(y0, _bf(ev_w_out[0]), h0, ev_ln_g[0].astype(f32), ev_ln_b[0].astype(f32),
                        tm=256, tn=512, with_bf16_copy=True)

    proj1 = _matmul(h1_bf, _bf(od_w_in[0]), f32, tm=1024, tn=1024)
    tz, bs, cs, l16r, l16i = _s5_operators(od_lam_re[0], od_lam_im[0], od_log_step[0], od_b_re[0], od_b_im[0],
                                           od_c_re[0], od_c_im[0])
    ys = _s5(proj1, tz, bs, cs, l16r, l16i, od_d[0].astype(f32), bsz, seq)
    y1 = _glu(ys, _bf(od_w_glu[0]), od_b_glu[0].astype(f32), proj1, tm=1024)
    h2, = _out_ln(y1, _bf(od_w_out[0]), h1, od_ln_g[0].astype(f32), od_ln_b[0].astype(f32),
                  tm=256, tn=512, with_bf16_copy=False)
    return h2.reshape(bsz, seq, d).astype(x.dtype)
```

```python
import functools
import math

import jax
import jax.numpy as jnp
from jax import lax
from jax.experimental import pallas as pl
from jax.experimental.pallas import tpu as pltpu

DEPTH = 2
ALPHA = (2 * DEPTH) ** 0.25
LN_EPS = 1e-5
RMS_EPS = 1e-6
LAMBDA_RE_MAX = -1e-4
CONV_K = 3
HG_HEADS = 16
HG_CHUNK = 32
S5_GROUP = 16
S5_STATE = 64

LANES = 128
SUBLANES = 8
VMEM_LIMIT = 61 * 1024 * 1024

MIX_TB = 256
MIX_HB = 16
S5_L = 16
S5_R = 128
S5_GB = 16
S5_RP = S5_R + SUBLANES
S5_LP = S5_L + SUBLANES
GLU_TN = 1024
GLU_ROWS = 256

_NT = (((1,), (1,)), ((), ()))


def _bf(x):
    return x.astype(jnp.bfloat16)


def _mm_kernel(a_ref, b_ref, o_ref, *maybe_a_bf):
    if maybe_a_bf:
        a_bf, = maybe_a_bf

        @pl.when(pl.program_id(1) == 0)
        def _():
            a_bf[...] = _bf(a_ref[...])
    else:
        a_bf = a_ref
    o_ref[...] = jnp.dot(a_bf[...], b_ref[...], preferred_element_type=jnp.float32).astype(o_ref.dtype)


def _matmul(a, b, out_dtype, tm, tn):
    m, k = a.shape
    n = b.shape[1]
    scratch = [] if a.dtype == jnp.bfloat16 else [pltpu.VMEM((tm, k), jnp.bfloat16)]
    return pl.pallas_call(
        _mm_kernel,
        grid=(m // tm, n // tn),
        in_specs=[pl.BlockSpec((tm, k), lambda i, j: (i, 0)),
                  pl.BlockSpec((k, tn), lambda i, j: (0, j))],
        out_specs=pl.BlockSpec((tm, tn), lambda i, j: (i, j)),
        out_shape=jax.ShapeDtypeStruct((m, n), out_dtype),
        scratch_shapes=scratch,
        compiler_params=pltpu.CompilerParams(
            dimension_semantics=("parallel", "arbitrary"), vmem_limit_bytes=VMEM_LIMIT),
        name="in_proj",
    )(a, b)


LN_ROWS = 4 * SUBLANES


def _out_ln_kernel(y_ref, w_ref, r_ref, g_ref, b_ref, o_ref, *maybe_obf, tn):
    eh = y_ref.shape[2]
    tm, d = o_ref.shape
    for c in range(d // tn):
        cols = slice(c * tn, (c + 1) * tn)
        acc = jnp.dot(y_ref[0], w_ref[:eh, cols], preferred_element_type=jnp.float32)
        acc += jnp.dot(y_ref[1], w_ref[eh:, cols], preferred_element_type=jnp.float32)
        o_ref[:, cols] = ALPHA * r_ref[:, cols] + acc

    def ln_slab(r0):
        for rb in range(LN_ROWS // SUBLANES):
            rows = slice(r0 + rb * SUBLANES, r0 + (rb + 1) * SUBLANES)
            z = o_ref[rows, :]
            mu = jnp.mean(z, axis=-1, keepdims=True)
            zc = z - mu
            var = jnp.mean(zc * zc, axis=-1, keepdims=True)
            o_ref[rows, :] = zc * lax.rsqrt(var + LN_EPS) * g_ref[...] + b_ref[...]
        if maybe_obf:
            maybe_obf[0][r0:r0 + LN_ROWS, :] = _bf(o_ref[r0:r0 + LN_ROWS, :])

    for r0 in range(0, tm, LN_ROWS):
        pl.when(pl.program_id(0) >= 0)(functools.partial(ln_slab, r0))


def _out_ln(y2, w, resid, g, b, tm, tn, with_bf16_copy):
    _, t, eh = y2.shape
    e, d = w.shape
    row_spec = pl.BlockSpec((tm, d), lambda i: (i, 0))
    out_shape = [jax.ShapeDtypeStruct((t, d), jnp.float32)]
    if with_bf16_copy:
        out_shape.append(jax.ShapeDtypeStruct((t, d), jnp.bfloat16))
    return pl.pallas_call(
        functools.partial(_out_ln_kernel, tn=tn),
        grid=(t // tm,),
        in_specs=[pl.BlockSpec((2, tm, eh), lambda i: (0, i, 0)),
                  pl.BlockSpec((e, d), lambda i: (0, 0), pipeline_mode=pl.Buffered(1)),
                  row_spec,
                  pl.BlockSpec((1, d), lambda i: (0, 0)),
                  pl.BlockSpec((1, d), lambda i: (0, 0))],
        out_specs=[row_spec] * len(out_shape),
        out_shape=out_shape,
        compiler_params=pltpu.CompilerParams(
            dimension_semantics=("arbitrary",), vmem_limit_bytes=VMEM_LIMIT),
        name="out_proj_ln",
    )(y2, w, resid, g.reshape(1, d), b.reshape(1, d))


def _chunk_cumsum(x, row_in_chunk):
    sh = 1
    while sh < HG_CHUNK:
        x = x + jnp.where(row_in_chunk >= sh, pltpu.roll(x, sh, 0), 0.0)
        sh *= 2
    return x


def _mixer0_kernel(ab_ref, ac_ref, ah_ref, q_ref, f_ref, v_ref, ga_ref, gb_ref,
                   cw_ref, lb_ref, nw_ref, o_ref, carry, state):
    tb = pl.program_id(2)
    tbk, w = ab_ref.shape
    nc = tbk // HG_CHUNK

    @pl.when(tb == 0)
    def _():
        carry[...] = jnp.zeros_like(carry)
        state[...] = jnp.zeros_like(state)

    p = ac_ref[...] * ah_ref[...]
    row = lax.broadcasted_iota(jnp.int32, (tbk, w), 0)
    c6 = carry[SUBLANES - 2:SUBLANES - 1, :]
    c7 = carry[SUBLANES - 1:SUBLANES, :]
    p1 = jnp.where(row == 0, c7, pltpu.roll(p, 1, 0))
    p2 = jnp.where(row == 0, c6, jnp.where(row == 1, c7, pltpu.roll(p, 2, 0)))
    conv = cw_ref[0:1, :] * p2 + cw_ref[1:2, :] * p1 + cw_ref[2:3, :] * p
    carry[...] = p[tbk - SUBLANES:, :]
    o_ref[0] = _bf(ab_ref[...] * conv * jax.nn.silu(ga_ref[...]))

    rowh = lax.broadcasted_iota(jnp.int32, (tbk, LANES), 0)
    ric = rowh % HG_CHUNK
    r2 = lax.broadcasted_iota(jnp.int32, (tbk, tbk), 0)
    c2 = lax.broadcasted_iota(jnp.int32, (tbk, tbk), 1)
    causal = (r2 // HG_CHUNK == c2 // HG_CHUNK) & (c2 <= r2)
    lane_chunk = lax.broadcasted_iota(jnp.int32, (LANES, tbk), 1) // HG_CHUNK
    outs = []
    for h in range(w // LANES):
        sl = slice(h * LANES, (h + 1) * LANES)
        q = q_ref[:, sl]
        v = v_ref[:, sl]
        lb = lb_ref[:, sl]
        f = lb + (1.0 - lb) * jax.nn.sigmoid(f_ref[:, sl])
        kk = 1.0 - f
        b = _chunk_cumsum(jnp.log(f), ric)
        b_last_rows = [b[(c + 1) * HG_CHUNK - 1:(c + 1) * HG_CHUNK, :] for c in range(nc)]
        b_last = jnp.concatenate([jnp.broadcast_to(r, (HG_CHUNK, LANES)) for r in b_last_rows], axis=0)
        q_in = _bf(q * jnp.exp(b))
        k_in = _bf(kk * jnp.exp(-b))
        k_dec = _bf(kk * jnp.exp(b_last - b))
        vb = _bf(v)
        s = lax.dot_general(q_in, k_in, _NT, preferred_element_type=jnp.float32)
        s = _bf(jnp.where(causal, s, 0.0))
        o_intra = jnp.dot(s, vb, preferred_element_type=jnp.float32)
        v_t = _bf(v.T)
        zero_t = jnp.zeros_like(v_t)
        v_stack = jnp.concatenate([jnp.where(lane_chunk == c, v_t, zero_t) for c in range(nc)], axis=0)
        upd = jnp.dot(v_stack, k_dec, preferred_element_type=jnp.float32)
        st = state[h]
        sts = []
        for c in range(nc):
            sts.append(_bf(st))
            st = st * jnp.exp(b_last_rows[c]) + upd[c * LANES:(c + 1) * LANES]
        state[h] = st
        st_all = jnp.concatenate(sts, axis=1)
        zero_q = jnp.zeros_like(q_in)
        q_exp = jnp.concatenate([jnp.where(rowh // HG_CHUNK == c, q_in, zero_q) for c in range(nc)], axis=1)
        o_inter = lax.dot_general(q_exp, st_all, _NT, preferred_element_type=jnp.float32)
        o = o_intra + o_inter
        o = o * lax.rsqrt(jnp.mean(o * o, axis=-1, keepdims=True) + RMS_EPS) * nw_ref[...]
        outs.append(o)
    ob = outs[0] if len(outs) == 1 else jnp.concatenate(outs, axis=1)
    o_ref[1] = _bf(ob * jax.nn.silu(gb_ref[...]))


def _mixer0(proj, conv_w, lb, hg_norm, bsz, seq):
    t, e_in = proj.shape
    cw = conv_w.shape[1]
    w = MIX_HB * LANES
    nsec = cw // w
    nt = seq // MIX_TB

    def sec(k):
        return pl.BlockSpec((MIX_TB, w), lambda b, g, s, k=k: (b * nt + s, k * nsec + g))

    return pl.pallas_call(
        _mixer0_kernel,
        grid=(bsz, nsec, nt),
        in_specs=[sec(0), sec(1), sec(2), sec(3), sec(4), sec(5), sec(6), sec(7),
                  pl.BlockSpec((CONV_K, w), lambda b, g, s: (0, g)),
                  pl.BlockSpec((1, w), lambda b, g, s: (0, g)),
                  pl.BlockSpec((1, LANES), lambda b, g, s: (0, 0))],
        out_specs=pl.BlockSpec((2, MIX_TB, w), lambda b, g, s: (0, b * nt + s, g)),
        out_shape=jax.ShapeDtypeStruct((2, t, cw), jnp.bfloat16),
        scratch_shapes=[pltpu.VMEM((SUBLANES, w), jnp.float32),
                        pltpu.VMEM((MIX_HB, LANES, LANES), jnp.float32)],
        compiler_params=pltpu.CompilerParams(
            dimension_semantics=("parallel", "parallel", "arbitrary"), vmem_limit_bytes=VMEM_LIMIT),
        name="mixer0",
    )(proj, proj, proj, proj, proj, proj, proj, proj, conv_w, lb.reshape(1, cw), hg_norm.reshape(1, LANES))


def _gelu_tanh(x):
    return 0.5 * x * (1.0 + jnp.tanh(math.sqrt(2.0 / math.pi) * (x + 0.044715 * (x * x * x))))


def _s5_kernel(*refs, nseq, nh):
    u_refs = refs[:nseq * nh]
    tz_ref, bs_ref, cs_ref, lr_ref, li_ref, d_ref, o_ref = refs[nseq * nh:nseq * nh + 7]
    scr = refs[nseq * nh + 7:]
    x_scr, y_scr, st_r, st_i = scr[:4]
    ar_scrs, ai_scrs, sr_scrs, si_scrs = (scr[4 + q * nseq:4 + (q + 1) * nseq] for q in range(4))
    o_scrs = scr[4 + 4 * nseq:4 + 4 * nseq + nseq * nh]
    u_keep = scr[4 + 4 * nseq + nseq * nh]
    gb = tz_ref.shape[0]
    r = S5_R
    rp = S5_RP
    npair = gb // 2
    gph = LANES // S5_GROUP
    p = S5_STATE

    @pl.when(pl.program_id(1) == 0)
    def _():
        st_r[...] = jnp.zeros_like(st_r)
        st_i[...] = jnp.zeros_like(st_i)
        for o_scr in o_scrs:
            o_scr[...] = jnp.zeros_like(o_scr)

    for b in range(nseq):
        for j in range(S5_L):
            for h in range(nh):
                rows = u_refs[b * nh + h][pl.ds(j, r, stride=S5_L), :]
                u_keep[(b * nh + h) * S5_L + j] = rows
                x_scr[b * gb + h * gph:b * gb + (h + 1) * gph, j * S5_GROUP:(j + 1) * S5_GROUP, :] = (
                    _bf(rows.T).reshape(gph, S5_GROUP, r))

    for b in range(nseq):
        for k in range(npair):
            a = [jnp.dot(bs_ref[g], x_scr[b * gb + g], preferred_element_type=jnp.float32)
                 for g in (2 * k, 2 * k + 1)]
            ar_scrs[b][k * rp:k * rp + r, :] = jnp.concatenate([a[0][:p], a[1][:p]], axis=0).T
            ai_scrs[b][k * rp:k * rp + r, :] = jnp.concatenate([a[0][p:], a[1][p:]], axis=0).T

    lr, li = lr_ref[...], li_ref[...]
    srs = [st_r[b] for b in range(nseq)]
    sis = [st_i[b] for b in range(nseq)]
    for n in range(r):
        for b in range(nseq):
            sr, si = srs[b], sis[b]
            sr_scrs[b][pl.ds(n, npair, stride=rp), :] = sr
            si_scrs[b][pl.ds(n, npair, stride=rp), :] = si
            a_r = ar_scrs[b][pl.ds(n, npair, stride=rp), :]
            a_i = ai_scrs[b][pl.ds(n, npair, stride=rp), :]
            srs[b], sis[b] = sr * lr - si * li + a_r, sr * li + si * lr + a_i
    for b in range(nseq):
        st_r[b] = srs[b]
        st_i[b] = sis[b]

    for b in range(nseq):
        for k in range(npair):
            srt = sr_scrs[b][k * rp:k * rp + r, :].T
            sit = si_scrs[b][k * rp:k * rp + r, :].T
            s_in = (_bf(jnp.concatenate([srt[:p], sit[:p]], axis=0)),
                    _bf(jnp.concatenate([srt[p:], sit[p:]], axis=0)))
            for q, g in enumerate((2 * k, 2 * k + 1)):
                y_scr[b * gb + g] = (jnp.dot(tz_ref[g], x_scr[b * gb + g], preferred_element_type=jnp.float32)
                                     + jnp.dot(cs_ref[g], s_in[q], preferred_element_type=jnp.float32))

    for b in range(nseq):
        for j in range(S5_L):
            for h in range(nh):
                yj = y_scr[b * gb + h * gph:b * gb + (h + 1) * gph, j * S5_GROUP:(j + 1) * S5_GROUP, :]
                yj = yj.reshape(LANES, r).T
                uj = u_keep[(b * nh + h) * S5_L + j]
                dsk = d_ref[:, h * LANES:(h + 1) * LANES]
                o_scrs[b * nh + h][pl.ds(j, r, stride=S5_LP), :] = _gelu_tanh(yj + dsk * uj)
        for h in range(nh):
            rows = o_scrs[b * nh + h][...].reshape(r, S5_LP, LANES)[:, :S5_L, :]
            o_ref[b, :, h * LANES:(h + 1) * LANES] = _bf(rows.reshape(r * S5_L, LANES))


def _s5(proj, tz, bs, cs, l16r, l16i, d_skip, bsz, seq):
    g_total = tz.shape[0]
    width = g_total * S5_GROUP
    ch = S5_GB * S5_GROUP
    nh = ch // LANES
    tile = S5_L * S5_R
    nt = seq // tile
    p2 = 2 * S5_STATE
    lc = S5_L * S5_GROUP
    npair = S5_GB // 2
    kern = functools.partial(_s5_kernel, nseq=bsz, nh=nh)
    u_specs = [pl.BlockSpec((tile, LANES), lambda g, i, b=b, h=h: (b * nt + i, g * nh + h))
               for b in range(bsz) for h in range(nh)]
    state_rows = pltpu.VMEM((npair * S5_RP, p2), jnp.float32)
    gpt = GLU_TN // ch
    out = pl.pallas_call(
        kern,
        grid=(g_total // S5_GB, nt),
        in_specs=u_specs + [
            pl.BlockSpec((S5_GB, lc, lc), lambda g, i: (g, 0, 0)),
            pl.BlockSpec((S5_GB, p2, lc), lambda g, i: (g, 0, 0)),
            pl.BlockSpec((S5_GB, lc, p2), lambda g, i: (g, 0, 0)),
            pl.BlockSpec((npair, p2), lambda g, i: (g, 0)),
            pl.BlockSpec((npair, p2), lambda g, i: (g, 0)),
            pl.BlockSpec((1, ch), lambda g, i: (0, g))],
        out_specs=pl.BlockSpec((None, bsz, tile, ch), lambda g, i: (g // gpt, 0, i, g % gpt)),
        out_shape=jax.ShapeDtypeStruct((width // GLU_TN, bsz, seq, GLU_TN), jnp.bfloat16),
        scratch_shapes=[pltpu.VMEM((bsz * S5_GB, lc, S5_R), jnp.bfloat16),
                        pltpu.VMEM((bsz * S5_GB, lc, S5_R), jnp.float32),
                        pltpu.VMEM((bsz, npair, p2), jnp.float32),
                        pltpu.VMEM((bsz, npair, p2), jnp.float32)]
        + [state_rows for _ in range(4 * bsz)]
        + [pltpu.VMEM((S5_R * S5_LP, LANES), jnp.float32) for _ in range(bsz * nh)]
        + [pltpu.VMEM((bsz * nh * S5_L, S5_R, LANES), jnp.float32)],
        compiler_params=pltpu.CompilerParams(
            dimension_semantics=("parallel", "arbitrary"), vmem_limit_bytes=VMEM_LIMIT),
        name="s5",
    )(*([proj] * (bsz * nh)), tz, bs, cs, l16r, l16i, d_skip.reshape(1, width))
    return out.reshape(width // GLU_TN, bsz * seq, GLU_TN)


def _s5_operators(lam_re, lam_im, log_step, b_re, b_im, c_re, c_im):
    f32 = jnp.float32
    hi = lax.Precision.HIGHEST
    g = lam_re.shape[0]
    lr = jnp.minimum(lam_re.astype(f32), LAMBDA_RE_MAX)
    li = lam_im.astype(f32)
    dt = jnp.exp(log_step.astype(f32))[:, None]
    nl, nc, npz = S5_L, S5_GROUP, S5_STATE
    lc = nl * nc
    a, w = lr * dt, li * dt

    def lam_pow(a_, w_, tau):
        mag = jnp.exp(a_ * tau)
        return mag * jnp.cos(w_ * tau), mag * jnp.sin(w_ * tau)

    p1r, p1i = lam_pow(a, w, 1.0)
    nr, ni = p1r - 1.0, p1i
    den = lr * lr + li * li
    kr, ki = (nr * lr + ni * li) / den, (ni * lr - nr * li) / den
    br, bi = b_re.astype(f32), b_im.astype(f32)
    bbr = kr[..., None] * br - ki[..., None] * bi
    bbi = kr[..., None] * bi + ki[..., None] * br
    cr, ci = c_re.astype(f32), c_im.astype(f32)
    per, pei = lam_pow(a[..., None], w[..., None], jnp.arange(nl - 1, -1, -1, dtype=f32))
    er = per[..., None] * bbr[:, :, None, :] - pei[..., None] * bbi[:, :, None, :]
    ei = per[..., None] * bbi[:, :, None, :] + pei[..., None] * bbr[:, :, None, :]
    bs = jnp.concatenate([er.reshape(g, npz, lc), ei.reshape(g, npz, lc)], axis=1)
    krev = jnp.einsum('gcq,gqx->gcx', jnp.concatenate([cr, -ci], axis=-1), bs, precision=hi)
    kpad = jnp.concatenate([krev, jnp.zeros((g, nc, lc - nc), f32)], axis=-1)
    tz = jnp.stack([kpad[:, :, (nl - 1 - j) * nc:(2 * nl - 1 - j) * nc] for j in range(nl)], axis=1)
    tz = tz.reshape(g, lc, lc)
    pcr, pci = lam_pow(a[:, None, :], w[:, None, :], jnp.arange(1, nl + 1, dtype=f32)[:, None])
    cpr = cr[:, None] * pcr[:, :, None, :] - ci[:, None] * pci[:, :, None, :]
    cpi = cr[:, None] * pci[:, :, None, :] + ci[:, None] * pcr[:, :, None, :]
    cs = jnp.concatenate([cpr, -cpi], axis=-1).reshape(g, lc, 2 * npz)
    plr, pli = lam_pow(a, w, float(nl))
    l16r = plr.reshape(g // 2, 2 * npz)
    l16i = pli.reshape(g // 2, 2 * npz)
    return _bf(tz), _bf(bs), _bf(cs), l16r, l16i


def _sigmoid_tanh(x):
    return 0.5 + 0.5 * jnp.tanh(0.5 * x)


def _glu_kernel(a_ref, w_ref, gt_ref, b_ref, o_ref, z_scr):
    nct, tm, tn = a_ref.shape
    j = pl.program_id(1)

    hm = GLU_ROWS
    for r0 in range(0, tm, hm):
        rows = slice(r0, r0 + hm)
        acc = jnp.dot(a_ref[0, rows, :], w_ref[:tn, :], preferred_element_type=jnp.float32)
        for c in range(1, nct):
            acc += jnp.dot(a_ref[c, rows, :], w_ref[c * tn:(c + 1) * tn, :], preferred_element_type=jnp.float32)
        z_scr[rows, :] = acc + b_ref[...]
        g = gt_ref[rows, :]
        gate = _sigmoid_tanh(z_scr[rows, :]) * (g * _sigmoid_tanh(g))
        o_ref[0, rows, :] = _bf(a_ref[j, rows, :].astype(jnp.float32) * gate)


def _glu(y_tiles, w, bias, proj, tm):
    nct, t, tn = y_tiles.shape
    e = nct * tn
    half = e // 2
    nh = half // tn
    goff = e // tn
    return pl.pallas_call(
        _glu_kernel,
        grid=(t // tm, e // tn),
        in_specs=[pl.BlockSpec((nct, tm, tn), lambda i, j: (0, i, 0)),
                  pl.BlockSpec((e, tn), lambda i, j: (0, j)),
                  pl.BlockSpec((tm, tn), lambda i, j: (i, goff + j)),
                  pl.BlockSpec((1, tn), lambda i, j: (0, j))],
        out_specs=pl.BlockSpec((1, tm, tn), lambda i, j: (j // nh, i, j % nh)),
        out_shape=jax.ShapeDtypeStruct((2, t, half), jnp.bfloat16),
        scratch_shapes=[pltpu.VMEM((tm, tn), jnp.float32)],
        compiler_params=pltpu.CompilerParams(
            dimension_semantics=("parallel", "arbitrary"), vmem_limit_bytes=VMEM_LIMIT),
        name="glu",
    )(y_tiles, w, proj, bias.reshape(1, e))


def kernel(x, ev_w_in, ev_conv_w, ev_hg_norm, ev_w_out, ev_ln_g, ev_ln_b, hg_lb_logits, od_w_in, od_lam_re,
           od_lam_im, od_log_step, od_b_re, od_b_im, od_c_re, od_c_im, od_d, od_w_glu, od_b_glu, od_w_out,
           od_ln_g, od_ln_b):
    bsz, seq, d = x.shape
    t = bsz * seq
    f32 = jnp.float32
    h0 = x.reshape(t, d).astype(f32)

    lb_all = jnp.cumsum(jax.nn.softmax(hg_lb_logits.astype(f32), axis=0), axis=0)

    proj0 = _matmul(h0, _bf(ev_w_in[0]), f32, tm=512, tn=1024)
    y0 = _mixer0(proj0, ev_conv_w[0].astype(f32), lb_all[0], ev_hg_norm[0].astype(f32), bsz, seq)
    h1, h1_bf = _out_ln(y0, _bf(ev_w_out[0]), h0, ev_ln_g[0].astype(f32), ev_ln_b[0].astype(f32),
                        tm=256, tn=512, with_bf16_copy=True)

    proj1 = _matmul(h1_bf, _bf(od_w_in[0]), f32, tm=1024, tn=1024)
    tz, bs, cs, l16r, l16i = _s5_operators(od_lam_re[0], od_lam_im[0], od_log_step[0], od_b_re[0], od_b_im[0],
                                           od_c_re[0], od_c_im[0])
    ys = _s5(proj1, tz, bs, cs, l16r, l16i, od_d[0].astype(f32), bsz, seq)
    y1 = _glu(ys, _bf(od_w_glu[0]), od_b_glu[0].astype(f32), proj1, tm=1024)
    h2, = _out_ln(y1, _bf(od_w_out[0]), h1, od_ln_g[0].astype(f32), od_ln_b[0].astype(f32),
                  tm=256, tn=512, with_bf16_copy=False)
    return h2.reshape(bsz, seq, d).astype(x.dtype)
```
